```python
import math
import jax, jax.numpy as jnp
from jax import lax
import numpy as np

D_MODEL = 2048
BATCH = 2
SEQ = 8192
DEPTH = 1

CTX_LEN = 256
GRID_W = 64
MIX_WIDTH = D_MODEL
MLSTM_HEAD_DIM = 128
MLSTM_HEADS = (MIX_WIDTH // 2) // MLSTM_HEAD_DIM
MLSTM_WIDTH = MLSTM_HEADS * MLSTM_HEAD_DIM
MLSTM_CHUNK = 128
DIFF_V_DIM = 128
DIFF_QK_DIM = DIFF_V_DIM // 2
DIFF_HEADS = (MIX_WIDTH - MLSTM_WIDTH) // DIFF_V_DIM
DIFF_WIDTH = DIFF_HEADS * DIFF_V_DIM
ROPE_FREQS = DIFF_QK_DIM // 4
ROPE_BASE = 10000.0
Q_BLOCK = 128
D_FF = 5632
CONV_W = 3
N_MOD = 6
EPS = 1e-6

OFF_MQ = 0
OFF_MK = MLSTM_WIDTH
OFF_MV = 2 * MLSTM_WIDTH
OFF_MO = 3 * MLSTM_WIDTH
OFF_MG = 4 * MLSTM_WIDTH
OFF_DQ = OFF_MG + 4 * MLSTM_HEADS
OFF_DK = OFF_DQ + DIFF_HEADS * 2 * DIFF_QK_DIM
OFF_DV = OFF_DK + DIFF_HEADS * 2 * DIFF_QK_DIM
IN_COLS = OFF_DV + DIFF_WIDTH

kernel_name = "hybrid_mlstm_diffattn_convffn_dit_block"


def rmsnorm(a, g):
    a32 = a.astype(jnp.float32)
    y = a32 * lax.rsqrt(jnp.mean(a32 * a32, axis=-1, keepdims=True) + EPS)
    return (y * g.astype(jnp.float32)).astype(a.dtype)


def modulate(h, shift, scale):
    return h * (1 + scale) + shift


def dwconv3(a, w, b):
    ap = jnp.pad(a, ((0, 0), (1, 1), (0, 0)))
    return ap[:, :-2] * w[0] + ap[:, 1:-1] * w[1] + ap[:, 2:] * w[2] + b


def axial_rope_tables(rows, n_lat):
    row = jnp.broadcast_to(jnp.arange(rows, dtype=jnp.float32)[:, None], (rows, GRID_W)).reshape(-1)
    col = jnp.broadcast_to(jnp.arange(GRID_W, dtype=jnp.float32)[None, :], (rows, GRID_W)).reshape(-1)
    inv_freq = ROPE_BASE ** (-jnp.arange(ROPE_FREQS, dtype=jnp.float32) / ROPE_FREQS)
    ang = jnp.stack([row[:, None] * inv_freq, col[:, None] * inv_freq], axis=1)
    ang = jnp.stack([ang, ang], axis=2).reshape(n_lat, 4 * ROPE_FREQS)
    return jnp.cos(ang), jnp.sin(ang)


def rope_2d(x, cos, sin):
    xr = x.reshape(*x.shape[:-1], 2, 2, ROPE_FREQS)
    rot = jnp.stack([-xr[..., 1, :], xr[..., 0, :]], axis=-2).reshape(x.shape)
    return x * cos[:, None, None, :] + rot * sin[:, None, None, :]


def to_mlstm_heads(a):
    B, T, _ = a.shape
    return a.reshape(B, T, MLSTM_HEADS, MLSTM_HEAD_DIM).transpose(0, 2, 1, 3).astype(jnp.float32)


def mlstm_project(p, conv_w, conv_b, b_gate):
    B, T, _ = p.shape
    qk = jax.nn.silu(dwconv3(p[..., OFF_MQ:OFF_MV], conv_w, conv_b))
    q = to_mlstm_heads(qk[..., :MLSTM_WIDTH])
    k = to_mlstm_heads(qk[..., MLSTM_WIDTH:]) * (MLSTM_HEAD_DIM ** -0.5)
    v = to_mlstm_heads(p[..., OFF_MV:OFF_MO])
    g = (p[..., OFF_MG:OFF_DQ].reshape(B, T, 4, MLSTM_HEADS) + b_gate).astype(jnp.float32)
    g = jnp.transpose(g, (2, 0, 3, 1))
    return q, k, v, g[0], jax.nn.log_sigmoid(g[1]), g[2], jax.nn.log_sigmoid(g[3])


def mlstm_chunkwise(q, k, v, i_pre, log_f, state):
    B, H, T, Dh = q.shape
    nc = T // MLSTM_CHUNK

    def to_chunks(a):
        return jnp.moveaxis(a.reshape(B, H, nc, MLSTM_CHUNK, *a.shape[3:]), 2, 0)

    xs = (to_chunks(q), to_chunks(k), to_chunks(v), to_chunks(i_pre), to_chunks(log_f))
    causal = jnp.tril(jnp.ones((MLSTM_CHUNK, MLSTM_CHUNK), dtype=bool))

    def step(carry, inp):
        C, n, m = carry
        qc, kc, vc, ic, fc = inp
        b = jnp.cumsum(fc, axis=-1)
        d_log = jnp.where(causal, b[..., :, None] - b[..., None, :] + ic[..., None, :], -jnp.inf)
        m_inter = b + m[..., None]
        m_t = jnp.maximum(m_inter, jnp.max(d_log, axis=-1))
        w_intra = jnp.exp(d_log - m_t[..., None])
        w_inter = jnp.exp(m_inter - m_t)
        s = jnp.einsum('bhld,bhsd->bhls', qc, kc) * w_intra
        num = w_inter[..., None] * jnp.einsum('bhld,bhde->bhle', qc, C) + jnp.einsum('bhls,bhse->bhle', s, vc)
        den = w_inter * jnp.einsum('bhld,bhd->bhl', qc, n) + jnp.sum(s, axis=-1)
        h = num / jnp.maximum(jnp.abs(den), jnp.exp(-m_t))[..., None]
        b_last = b[..., -1]
        log_w = b_last[..., None] - b + ic
        m_new = jnp.maximum(b_last + m, jnp.max(log_w, axis=-1))
        w_s = jnp.exp(log_w - m_new[..., None])
        decay = jnp.exp(b_last + m - m_new)
        C_new = decay[..., None, None] * C + jnp.einsum('bhs,bhsd,bhse->bhde', w_s, kc, vc)
        n_new = decay[..., None] * n + jnp.einsum('bhs,bhsd->bhd', w_s, kc)
        return (C_new, n_new, m_new), h

    state, hs = lax.scan(step, state, xs)
    return jnp.moveaxis(hs, 0, 2).reshape(B, H, T, Dh), state


def mlstm_output(h, p, g_norm):
    B, H, T, Dh = h.shape
    h = rmsnorm(h.transpose(0, 2, 1, 3), g_norm.reshape(H, Dh)).reshape(B, T, H * Dh)
    return h * jax.nn.sigmoid(p[..., OFF_MO:OFF_MG].astype(jnp.float32))


def mlstm_group(p_lat, p_ctx, conv_w, conv_b, b_gate, g_norm, need_ctx):
    ql, kl, vl, ifl, lfl, ibl, lbl = mlstm_project(p_lat, conv_w, conv_b, b_gate)
    qc, kc, vc, ifc, lfc, ibc, lbc = mlstm_project(p_ctx, conv_w, conv_b, b_gate)
    B = p_lat.shape[0]
    zero = (jnp.zeros((B, MLSTM_HEADS, MLSTM_HEAD_DIM, MLSTM_HEAD_DIM), jnp.float32),
            jnp.zeros((B, MLSTM_HEADS, MLSTM_HEAD_DIM), jnp.float32),
            jnp.zeros((B, MLSTM_HEADS), jnp.float32))

    def rev(*arrs):
        return tuple(jnp.flip(a, axis=2) for a in arrs)

    hc_f, st_f = mlstm_chunkwise(qc, kc, vc, ifc, lfc, zero)
    hl_f, _ = mlstm_chunkwise(ql, kl, vl, ifl, lfl, st_f)
    hc_b, st_b = mlstm_chunkwise(*rev(qc, kc, vc, ibc, lbc), zero)
    hl_b, _ = mlstm_chunkwise(*rev(ql, kl, vl, ibl, lbl), st_b)
    out_lat = mlstm_output(hl_f + jnp.flip(hl_b, axis=2), p_lat, g_norm)
    out_ctx = mlstm_output(hc_f + jnp.flip(hc_b, axis=2), p_ctx, g_norm) if need_ctx else None
    return out_lat, out_ctx


def diff_split(p):
    B, T, _ = p.shape
    q = p[..., OFF_DQ:OFF_DK].reshape(B, T, DIFF_HEADS, 2, DIFF_QK_DIM)
    k = p[..., OFF_DK:OFF_DV].reshape(B, T, DIFF_HEADS, 2, DIFF_QK_DIM)
    v = p[..., OFF_DV:IN_COLS].reshape(B, T, DIFF_HEADS, DIFF_V_DIM)
    return q, k, v


def diff_attend(q, k, v, lam):
    B, Tq = q.shape[:2]
    qb = jnp.moveaxis(q.astype(jnp.float32).reshape(B, Tq // Q_BLOCK, Q_BLOCK, *q.shape[2:]), 1, 0)
    k32 = k.astype(jnp.float32)
    v32 = v.astype(jnp.float32)
    scale = DIFF_QK_DIM ** -0.5

    def one_block(qblk):
        s = jnp.einsum('bqhcd,bkhcd->bhcqk', qblk, k32) * scale
        pr = jax.nn.softmax(s, axis=-1)
        pd = pr[:, :, 0] - lam * pr[:, :, 1]
        return jnp.einsum('bhqk,bkhe->bqhe', pd, v32)

    o = lax.map(one_block, qb)
    return jnp.moveaxis(o, 0, 1).reshape(B, Tq, DIFF_HEADS, DIFF_V_DIM)


def diff_output(o, g_norm, lam_init):
    B, T = o.shape[:2]
    return (rmsnorm(o, g_norm) * (1.0 - lam_init)).reshape(B, T, DIFF_WIDTH)


def diff_group(p_lat, p_ctx, cos, sin, lam, lam_init, g_norm, need_ctx):
    ql, kl, vl = diff_split(p_lat)
    qc, kc, vc = diff_split(p_ctx)
    ql = rope_2d(ql, cos, sin)
    kl = rope_2d(kl, cos, sin)
    k_all = jnp.concatenate([kc.astype(jnp.float32), kl.astype(jnp.float32)], axis=1)
    v_all = jnp.concatenate([vc, vl], axis=1)
    out_lat = diff_output(diff_attend(ql, k_all, v_all, lam), g_norm, lam_init)
    out_ctx = diff_output(diff_attend(qc, kc, vc, lam), g_norm, lam_init) if need_ctx else None
    return out_lat, out_ctx


def conv_ffn(h, w_up, conv_w, conv_b, w_down):
    u = dwconv3(h @ w_up, conv_w, conv_b)
    gate, val = jnp.split(u, 2, axis=-1)
    return (jax.nn.silu(gate) * val) @ w_down


def setup_inputs(seed: int = 0) -> dict:
    key = jax.random.key(seed)
    ks = jax.random.split(key, 32)
    f32 = jnp.float32

    def nrm(k, shape, s):
        return jax.random.normal(k, shape, f32) * s

    x = nrm(ks[0], (BATCH, SEQ, D_MODEL), 1.0)
    c = nrm(ks[1], (BATCH, D_MODEL), 1.0)
    ctx = nrm(ks[2], (BATCH, CTX_LEN, D_MODEL), 1.0)
    c_ctx = nrm(ks[3], (D_MODEL,), 1.0)
    w_mod = nrm(ks[4], (DEPTH, D_MODEL, N_MOD * D_MODEL), 0.5 * D_MODEL ** -0.5)
    b_mod = nrm(ks[5], (DEPTH, N_MOD * D_MODEL), 0.02)
    g_pre_mix = 1.0 + nrm(ks[6], (DEPTH, D_MODEL), 0.05)
    g_post_mix = 1.0 + nrm(ks[7], (DEPTH, D_MODEL), 0.05)
    w_in = nrm(ks[8], (DEPTH, D_MODEL, IN_COLS), D_MODEL ** -0.5)
    i_bias = nrm(ks[9], (DEPTH, 2, MLSTM_HEADS), 0.1)
    f_bias = 3.0 + 3.0 * jax.random.uniform(ks[10], (DEPTH, 2, MLSTM_HEADS), f32)
    b_gate = jnp.stack([i_bias[:, 0], f_bias[:, 0], i_bias[:, 1], f_bias[:, 1]], axis=1)
    conv_qk_w = nrm(ks[11], (DEPTH, CONV_W, 2 * MLSTM_WIDTH), CONV_W ** -0.5)
    conv_qk_b = nrm(ks[12], (DEPTH, 2 * MLSTM_WIDTH), 0.02)
    g_mlstm = 1.0 + nrm(ks[13], (DEPTH, MLSTM_WIDTH), 0.05)
    lambda_q1 = nrm(ks[14], (DEPTH, DIFF_QK_DIM), 0.1)
    lambda_k1 = nrm(ks[15], (DEPTH, DIFF_QK_DIM), 0.1)
    lambda_q2 = nrm(ks[16], (DEPTH, DIFF_QK_DIM), 0.1)
    lambda_k2 = nrm(ks[17], (DEPTH, DIFF_QK_DIM), 0.1)
    g_diff = 1.0 + nrm(ks[18], (DEPTH, DIFF_V_DIM), 0.05)
    w_out = nrm(ks[19], (DEPTH, MIX_WIDTH, D_MODEL), MIX_WIDTH ** -0.5)
    g_pre_ffn = 1.0 + nrm(ks[20], (DEPTH, D_MODEL), 0.05)
    g_post_ffn = 1.0 + nrm(ks[21], (DEPTH, D_MODEL), 0.05)
    w_up = nrm(ks[22], (DEPTH, D_MODEL, 2 * D_FF), D_MODEL ** -0.5)
    conv_ffn_w = nrm(ks[23], (DEPTH, CONV_W, 2 * D_FF), CONV_W ** -0.5)
    conv_ffn_b = nrm(ks[24], (DEPTH, 2 * D_FF), 0.02)
    w_down = nrm(ks[25], (DEPTH, D_FF, D_MODEL), D_FF ** -0.5)
    return {"x": x, "c": c, "ctx": ctx, "c_ctx": c_ctx, "w_mod": w_mod, "b_mod": b_mod,
            "g_pre_mix": g_pre_mix, "g_post_mix": g_post_mix, "w_in": w_in, "b_gate": b_gate,
            "conv_qk_w": conv_qk_w, "conv_qk_b": conv_qk_b, "g_mlstm": g_mlstm,
            "lambda_q1": lambda_q1, "lambda_k1": lambda_k1, "lambda_q2": lambda_q2, "lambda_k2": lambda_k2,
            "g_diff": g_diff, "w_out": w_out, "g_pre_ffn": g_pre_ffn, "g_post_ffn": g_post_ffn,
            "w_up": w_up, "conv_ffn_w": conv_ffn_w, "conv_ffn_b": conv_ffn_b, "w_down": w_down}


def reference(x, c, ctx, c_ctx, w_mod, b_mod, g_pre_mix, g_post_mix, w_in, b_gate,
              conv_qk_w, conv_qk_b, g_mlstm, lambda_q1, lambda_k1, lambda_q2, lambda_k2,
              g_diff, w_out, g_pre_ffn, g_post_ffn, w_up, conv_ffn_w, conv_ffn_b, w_down):
    n_lat = x.shape[1]
    ROWS = n_lat // GRID_W
    cos, sin = axial_rope_tables(ROWS, n_lat)
    for l in range(DEPTH):
        need_ctx = l < DEPTH - 1
        mod = jax.nn.silu(c) @ w_mod[l] + b_mod[l]
        sh1, sc1, ga1, sh2, sc2, ga2 = jnp.split(mod[:, None, :], N_MOD, axis=-1)
        cmod = jax.nn.silu(c_ctx) @ w_mod[l] + b_mod[l]
        csh1, csc1, cga1, csh2, csc2, cga2 = jnp.split(cmod, N_MOD, axis=-1)

        p_lat = modulate(rmsnorm(x, g_pre_mix[l]), sh1, sc1) @ w_in[l]
        p_ctx = modulate(rmsnorm(ctx, g_pre_mix[l]), csh1, csc1) @ w_in[l]
        m_lat, m_ctx = mlstm_group(p_lat, p_ctx, conv_qk_w[l], conv_qk_b[l], b_gate[l], g_mlstm[l], need_ctx)
        lam_init = 0.8 - 0.6 * math.exp(-0.3 * l)
        lam = (jnp.exp(jnp.sum(lambda_q1[l].astype(jnp.float32) * lambda_k1[l].astype(jnp.float32)))
               - jnp.exp(jnp.sum(lambda_q2[l].astype(jnp.float32) * lambda_k2[l].astype(jnp.float32)))
               + lam_init)
        d_lat, d_ctx = diff_group(p_lat, p_ctx, cos, sin, lam, lam_init, g_diff[l], need_ctx)
        y_lat = jnp.concatenate([m_lat, d_lat], axis=-1).astype(x.dtype) @ w_out[l]
        x = x + ga1 * rmsnorm(y_lat, g_post_mix[l])

        h_lat = modulate(rmsnorm(x, g_pre_ffn[l]), sh2, sc2)
        x = x + ga2 * rmsnorm(conv_ffn(h_lat, w_up[l], conv_ffn_w[l], conv_ffn_b[l], w_down[l]), g_post_ffn[l])

        if need_ctx:
            y_ctx = jnp.concatenate([m_ctx, d_ctx], axis=-1).astype(ctx.dtype) @ w_out[l]
            ctx = ctx + cga1 * rmsnorm(y_ctx, g_post_mix[l])
            h_ctx = modulate(rmsnorm(ctx, g_pre_ffn[l]), csh2, csc2)
            ctx = ctx + cga2 * rmsnorm(conv_ffn(h_ctx, w_up[l], conv_ffn_w[l], conv_ffn_b[l], w_down[l]), g_post_ffn[l])
    return x
```

```python
import functools
import math

import jax
import jax.numpy as jnp
from jax import lax
from jax.experimental import pallas as pl
from jax.experimental.pallas import tpu as pltpu

F32 = jnp.float32
BF16 = jnp.bfloat16

D_MODEL = 2048
CTX_LEN = 256
GRID_W = 64
HEADS = 8
HEAD_DIM = 128
GROUP_WIDTH = HEADS * HEAD_DIM
CHUNK = 128
DIFF_QK_DIM = 64
ROPE_FREQS = 16
ROPE_BASE = 10000.0
D_FF = 5632
N_MOD = 6
EPS = 1e-6
N_GATES = 4 * HEADS

OFF_MG = 4 * GROUP_WIDTH
OFF_DQ = OFF_MG + N_GATES

LANES = 128
BF16_SUBLANES = 16
VMEM_LIMIT = 56 * 1024 * 1024

TILE_MQ, TILE_MK, TILE_MV, TILE_MO, TILE_DQ, TILE_DK, TILE_DV = range(7)
N_PROJ_TILES = 7
P_TILES = 6

ATT_TQ = 512
ATT_TK = 512


def _cparams(sem):
    return pltpu.CompilerParams(dimension_semantics=sem, vmem_limit_bytes=VMEM_LIMIT)


def _dot(a, b):
    return jnp.dot(a, b, preferred_element_type=F32)


def _dot_nt(a, b):
    return lax.dot_general(a, b, (((1,), (1,)), ((), ())), preferred_element_type=F32)


def _rms(y, g):
    return y * lax.rsqrt(jnp.mean(y * y, axis=-1, keepdims=True) + EPS) * g


def _mod_kernel(c_ref, w_ref, b_ref, o_ref):
    cc = c_ref[...]
    s = cc * jax.nn.sigmoid(cc)
    o_ref[...] = _dot(s.astype(BF16), w_ref[...].astype(BF16)) + b_ref[...]


def _mod_call(cc, w_mod, b_mod):
    n = w_mod.shape[1]
    tn = 1024
    return pl.pallas_call(
        _mod_kernel,
        grid=(n // tn,),
        in_specs=[pl.BlockSpec((8, D_MODEL), lambda j: (0, 0)),
                  pl.BlockSpec((D_MODEL, tn), lambda j: (0, j)),
                  pl.BlockSpec((1, tn), lambda j: (0, j))],
        out_specs=pl.BlockSpec((8, tn), lambda j: (0, j)),
        out_shape=jax.ShapeDtypeStruct((8, n), F32),
        compiler_params=_cparams(("parallel",)),
        name="mod",
    )(cc, w_mod, b_mod)


def _rope(acc, cos, sin_signed, first_half):
    out = []
    for c in range(acc.shape[1] // LANES):
        a = acc[:, c * LANES:(c + 1) * LANES]
        rot = jnp.where(first_half, pltpu.roll(a, LANES - ROPE_FREQS, 1), pltpu.roll(a, ROPE_FREQS, 1))
        out.append(a * cos + rot * sin_signed)
    return jnp.concatenate(out, axis=1)


def _inproj_kernel(x_ref, g_ref, sh_ref, sc_ref, w_ref, wgt_ref, cos_ref, sin_ref,
                   p_ref, vt_ref, gt_ref, xn_ref, *, rope, tk_v):
    j = pl.program_id(1)
    tm = x_ref.shape[0]

    @pl.when(j == 0)
    def _():
        x = x_ref[...]
        y = _rms(x, g_ref[...]) * (1.0 + sc_ref[0]) + sh_ref[0]
        xn = y.astype(BF16)
        xn_ref[...] = xn
        gt = _dot_nt(wgt_ref[...], xn)
        for c in range(tm // CHUNK):
            gt_ref[c] = gt[:, c * CHUNK:(c + 1) * CHUNK]

    acc = _dot(xn_ref[...], w_ref[...])

    is_rope = (j == TILE_DQ) | (j == TILE_DK)
    is_v = j == TILE_DV

    if rope:
        @pl.when(is_rope)
        def _():
            lane = lax.broadcasted_iota(jnp.int32, (tm, LANES), 1)
            first_half = (lane % (2 * ROPE_FREQS)) < ROPE_FREQS
            p_ref[...] = _rope(acc, cos_ref[...], sin_ref[...], first_half).astype(BF16)

        plain = jnp.logical_not(is_rope | is_v)
    else:
        plain = jnp.logical_not(is_v)

    @pl.when(plain)
    def _():
        p_ref[...] = acc.astype(BF16)

    @pl.when(is_v)
    def _():
        at = acc.T.astype(BF16)
        for c in range(tm // tk_v):
            vt_ref[c] = at[:, c * tk_v:(c + 1) * tk_v]


def _inproj_call(x2d, g, sh, sc, wp, wgt, cos_t, sin_t, *, tm, rows_per_batch, rope, tk_v):
    m = x2d.shape[0]
    blocks_per_batch = rows_per_batch // tm
    kern = functools.partial(_inproj_kernel, rope=rope, tk_v=tk_v)
    return pl.pallas_call(
        kern,
        grid=(m // tm, N_PROJ_TILES),
        in_specs=[
            pl.BlockSpec((tm, D_MODEL), lambda i, j: (i, 0)),
            pl.BlockSpec((1, D_MODEL), lambda i, j: (0, 0)),
            pl.BlockSpec((1, 1, D_MODEL), lambda i, j: (i // blocks_per_batch, 0, 0)),
            pl.BlockSpec((1, 1, D_MODEL), lambda i, j: (i // blocks_per_batch, 0, 0)),
            pl.BlockSpec((D_MODEL, GROUP_WIDTH), lambda i, j: (0, j)),
            pl.BlockSpec((N_GATES, D_MODEL), lambda i, j: (0, 0)),
            pl.BlockSpec((tm, LANES), lambda i, j: (i % blocks_per_batch, 0)),
            pl.BlockSpec((tm, LANES), lambda i, j: (i % blocks_per_batch, 0)),
        ],
        out_specs=[
            pl.BlockSpec((tm, GROUP_WIDTH), lambda i, j: (i, jnp.minimum(j, P_TILES - 1))),
            pl.BlockSpec((tm // tk_v, GROUP_WIDTH, tk_v), lambda i, j: (i, 0, 0)),
            pl.BlockSpec((tm // CHUNK, N_GATES, CHUNK), lambda i, j: (i, 0, 0)),
        ],
        out_shape=[
            jax.ShapeDtypeStruct((m, P_TILES * GROUP_WIDTH), BF16),
            jax.ShapeDtypeStruct((m // tk_v, GROUP_WIDTH, tk_v), BF16),
            jax.ShapeDtypeStruct((m // CHUNK, N_GATES, CHUNK), F32),
        ],
        scratch_shapes=[pltpu.VMEM((tm, D_MODEL), BF16)],
        compiler_params=_cparams(("parallel", "arbitrary")),
        name="inproj_rope" if rope else "inproj_ctx",
    )(x2d, g, sh, sc, wp, wgt, cos_t, sin_t)


def _log_sigmoid(x):
    return jnp.minimum(x, 0.0) - jnp.log1p(jnp.exp(-jnp.abs(x)))


def _split_dot(row, tri):
    hi = row.astype(BF16)
    lo = (row - hi.astype(F32)).astype(BF16)
    hi8 = jnp.broadcast_to(hi, (BF16_SUBLANES, CHUNK))
    lo8 = jnp.broadcast_to(lo, (BF16_SUBLANES, CHUNK))
    return (_dot(hi8, tri) + _dot(lo8, tri))[0:1, :]


def _mlstm_kernel(bg_ref,
                  q_ref, k_ref, v_ref, o_ref, gt_ref,
                  kc_ref, vc_ref, gtc_ref,
                  cwq_ref, cwk_ref, cbq_ref, cbk_ref, gn_ref,
                  out_ref,
                  qs_ref, kst_ref, gs_ref, kstc_ref, gsc_ref, hf_ref, hb_ref, cst_ref):
    h = pl.program_id(1)
    n_lat = q_ref.shape[0] // CHUNK
    n_ctx = kc_ref.shape[0] // CHUNK

    row_i = lax.broadcasted_iota(jnp.int32, (CHUNK, CHUNK), 0)
    col_i = lax.broadcasted_iota(jnp.int32, (CHUNK, CHUNK), 1)
    tril = col_i <= row_i
    triu = col_i >= row_i
    tril_b = tril.astype(BF16)
    triu_b = triu.astype(BF16)
    ones_col = (col_i == 0).astype(BF16)
    first_row = row_i == 0
    last_row = row_i == CHUNK - 1

    def conv_silu(ref, c, n_chunks, w_ref, b_ref):
        r0 = pl.multiple_of(c * CHUNK, CHUNK)
        main = ref[pl.ds(r0, CHUNK), :].astype(F32)
        pstart = pl.multiple_of(jnp.maximum(r0 - BF16_SUBLANES, 0), BF16_SUBLANES)
        nstart = pl.multiple_of(jnp.minimum(r0 + CHUNK, (n_chunks - 1) * CHUNK), BF16_SUBLANES)
        prev_blk = ref[pl.ds(pstart, BF16_SUBLANES), :].astype(F32)
        next_blk = ref[pl.ds(nstart, BF16_SUBLANES), :].astype(F32)
        prev_row = prev_blk[BF16_SUBLANES - 1:BF16_SUBLANES, :] * jnp.where(c > 0, 1.0, 0.0)
        next_row = next_blk[0:1, :] * jnp.where(c < n_chunks - 1, 1.0, 0.0)
        up = jnp.where(first_row, prev_row, pltpu.roll(main, 1, 0))
        dn = jnp.where(last_row, next_row, pltpu.roll(main, CHUNK - 1, 0))
        w = w_ref[...]
        u = up * w[0:1, :] + main * w[1:2, :] + dn * w[2:3, :] + b_ref[...]
        return u * jax.nn.sigmoid(u)

    def gate_rows(g_ref3, dst_ref, c):
        def row(t):
            return g_ref3[c, pl.ds(t * HEADS + h, 1), :] + bg_ref[t, h]
        i_f, f_f, i_b, f_b = row(0), row(1), row(2), row(3)
        lf_f = _log_sigmoid(f_f)
        lf_b = _log_sigmoid(f_b)
        dst_ref[c, 0:1, :] = i_f - _split_dot(lf_f, triu_b)
        dst_ref[c, 1:2, :] = lf_f
        dst_ref[c, 2:3, :] = i_b - _split_dot(lf_b, tril_b)
        dst_ref[c, 3:4, :] = lf_b

    k_scale = HEAD_DIM ** -0.5

    def prep_lat(c, carry):
        qs_ref[c] = conv_silu(q_ref, c, n_lat, cwq_ref, cbq_ref).astype(BF16)
        kk = conv_silu(k_ref, c, n_lat, cwk_ref, cbk_ref) * k_scale
        kst_ref[c] = kk.T.astype(BF16)
        gate_rows(gt_ref, gs_ref, c)
        return carry

    lax.fori_loop(0, n_lat, prep_lat, 0)

    def prep_ctx(c, carry):
        kk = conv_silu(kc_ref, c, n_ctx, cwk_ref, cbk_ref) * k_scale
        kstc_ref[c] = kk.T.astype(BF16)
        gate_rows(gtc_ref, gsc_ref, c)
        return carry

    lax.fori_loop(0, n_ctx, prep_ctx, 0)

    def vaug(vref, c):
        r0 = pl.multiple_of(c * CHUNK, CHUNK)
        return jnp.concatenate([vref[pl.ds(r0, CHUNK), :], ones_col], axis=1)

    def state_update(d, m_prev, a_row, lf_row, kt, va):
        m_last = jnp.maximum(m_prev, jnp.max(a_row, axis=1, keepdims=True))
        b_last = jnp.sum(lf_row, axis=1, keepdims=True)
        w_row = jnp.exp(a_row - m_last)
        decay = jnp.exp(m_prev - m_last)
        kw = (kt.astype(F32) * w_row).astype(BF16)
        cst_ref[d] = decay * cst_ref[d] + _dot(kw, va)
        return b_last + m_last

    def chunk_step(d, c, m_prev, mask):
        a_row = gs_ref[c, 2 * d:2 * d + 1, :]
        lf_row = gs_ref[c, 2 * d + 1:2 * d + 2, :]
        q = qs_ref[c]
        kt = kst_ref[c]
        va = vaug(v_ref, c)
        a_mat = jnp.where(mask, a_row, -jnp.inf)
        m_col = jnp.maximum(jnp.max(a_mat, axis=1, keepdims=True), m_prev)
        w_mat = jnp.exp(a_mat - m_col)
        w_inter = jnp.exp(m_prev - m_col)
        b_col = jnp.sum(jnp.where(mask, lf_row, 0.0), axis=1, keepdims=True)
        s = (_dot(q, kt) * w_mat).astype(BF16)
        nd = w_inter * _dot(q, cst_ref[d].astype(BF16)) + _dot(s, va)
        num = nd[:, :HEAD_DIM]
        den = nd[:, HEAD_DIM:HEAD_DIM + 1]
        hh = num / jnp.maximum(jnp.abs(den), jnp.exp(-b_col - m_col))
        m_new = state_update(d, m_prev, a_row, lf_row, kt, va)
        return hh, m_new

    cst_ref[...] = jnp.zeros(cst_ref.shape, F32)

    def ctx_fwd(c, m):
        return state_update(0, m, gsc_ref[c, 0:1, :], gsc_ref[c, 1:2, :], kstc_ref[c], vaug(vc_ref, c))

    def ctx_bwd(i, m):
        c = n_ctx - 1 - i
        return state_update(1, m, gsc_ref[c, 2:3, :], gsc_ref[c, 3:4, :], kstc_ref[c], vaug(vc_ref, c))

    m_f = lax.fori_loop(0, n_ctx, ctx_fwd, jnp.zeros((1, 1), F32))
    m_b = lax.fori_loop(0, n_ctx, ctx_bwd, jnp.zeros((1, 1), F32))

    def main_step(i, carry):
        m_f, m_b = carry
        hf, m_f = chunk_step(0, i, m_f, tril)
        hf_ref[i] = hf
        cb = n_lat - 1 - i
        hb, m_b = chunk_step(1, cb, m_b, triu)
        hb_ref[cb] = hb
        return m_f, m_b

    lax.fori_loop(0, n_lat, main_step, (m_f, m_b))

    def finish(c, carry):
        r0 = pl.multiple_of(c * CHUNK, CHUNK)
        hh = _rms(hf_ref[c] + hb_ref[c], gn_ref[...])
        og = o_ref[pl.ds(r0, CHUNK), :].astype(F32)
        out_ref[pl.ds(r0, CHUNK), :] = (hh * jax.nn.sigmoid(og)).astype(BF16)
        return carry

    lax.fori_loop(0, n_lat, finish, 0)


def _mlstm_call(bg, p_lat, gt_lat, p_ctx, gt_ctx, cw, cb, gn, *, batch, seq):
    n_lat = seq // CHUNK
    n_ctx = CTX_LEN // CHUNK

    def col(tile):
        return lambda b, h: (b, tile * HEADS + h)

    lat_blk = (seq, HEAD_DIM)
    ctx_blk = (CTX_LEN, HEAD_DIM)
    return pl.pallas_call(
        _mlstm_kernel,
        grid=(batch, HEADS),
        in_specs=[
            pl.BlockSpec(memory_space=pltpu.SMEM),
            pl.BlockSpec(lat_blk, col(TILE_MQ)),
            pl.BlockSpec(lat_blk, col(TILE_MK)),
            pl.BlockSpec(lat_blk, col(TILE_MV)),
            pl.BlockSpec(lat_blk, col(TILE_MO)),
            pl.BlockSpec((n_lat, N_GATES, CHUNK), lambda b, h: (b, 0, 0)),
            pl.BlockSpec(ctx_blk, col(TILE_MK)),
            pl.BlockSpec(ctx_blk, col(TILE_MV)),
            pl.BlockSpec((n_ctx, N_GATES, CHUNK), lambda b, h: (b, 0, 0)),
            pl.BlockSpec((3, HEAD_DIM), lambda b, h: (0, h)),
            pl.BlockSpec((3, HEAD_DIM), lambda b, h: (0, HEADS + h)),
            pl.BlockSpec((1, HEAD_DIM), lambda b, h: (0, h)),
            pl.BlockSpec((1, HEAD_DIM), lambda b, h: (0, HEADS + h)),
            pl.BlockSpec((1, HEAD_DIM), lambda b, h: (0, h)),
        ],
        out_specs=pl.BlockSpec(lat_blk, lambda b, h: (b, h)),
        out_shape=jax.ShapeDtypeStruct((batch * seq, GROUP_WIDTH), BF16),
        scratch_shapes=[
            pltpu.VMEM((n_lat, CHUNK, HEAD_DIM), BF16),
            pltpu.VMEM((n_lat, HEAD_DIM, CHUNK), BF16),
            pltpu.VMEM((n_lat, 8, CHUNK), F32),
            pltpu.VMEM((n_ctx, HEAD_DIM, CHUNK), BF16),
            pltpu.VMEM((n_ctx, 8, CHUNK), F32),
            pltpu.VMEM((n_lat, CHUNK, HEAD_DIM), F32),
            pltpu.VMEM((n_lat, CHUNK, HEAD_DIM), F32),
            pltpu.VMEM((2, HEAD_DIM, 2 * HEAD_DIM), F32),
        ],
        compiler_params=_cparams(("parallel", "parallel")),
        name="mlstm",
    )(bg, p_lat, p_lat, p_lat, p_lat, gt_lat, p_ctx, p_ctx, gt_ctx, cw, cw, cb, cb, gn)


def _attn_kernel(q_ref, k_ref, vt_ref, kc_ref, vtc_ref, lam_ref, gd_ref, o_ref,
                 qq_ref, m_ref, l_ref, acc_ref, *, lam_init):
    tq = q_ref.shape[0]
    n_kv = k_ref.shape[0] // ATT_TK

    lane = lax.broadcasted_iota(jnp.int32, (tq, HEAD_DIM), 1)
    q = q_ref[...]
    zero = jnp.zeros_like(q)
    scale = DIFF_QK_DIM ** -0.5
    qq_ref[0:tq, :] = jnp.where(lane < DIFF_QK_DIM, q, zero) * scale
    qq_ref[tq:2 * tq, :] = jnp.where(lane >= DIFF_QK_DIM, q, zero) * scale
    m_ref[...] = jnp.full(m_ref.shape, -jnp.inf, F32)
    l_ref[...] = jnp.zeros(l_ref.shape, F32)
    acc_ref[...] = jnp.zeros(acc_ref.shape, F32)

    def step(k, vt):
        st = _dot_nt(k, qq_ref[...])
        m_prev = m_ref[...]
        m_new = jnp.maximum(m_prev, jnp.max(st, axis=0, keepdims=True))
        p = jnp.exp(st - m_new)
        alpha = jnp.exp(m_prev - m_new)
        l_ref[...] = alpha * l_ref[...] + jnp.sum(p, axis=0, keepdims=True)
        acc_ref[...] = alpha * acc_ref[...] + _dot(vt, p.astype(BF16))
        m_ref[...] = m_new

    step(kc_ref[...], vtc_ref[0])

    def body(kv, carry):
        r0 = pl.multiple_of(kv * ATT_TK, ATT_TK)
        step(k_ref[pl.ds(r0, ATT_TK), :], vt_ref[kv])
        return carry

    lax.fori_loop(0, n_kv, body, 0)

    lp = lam_ref[...]
    e1 = jnp.exp(jnp.sum(lp[0:1, :] * lp[1:2, :], axis=1, keepdims=True))
    e2 = jnp.exp(jnp.sum(lp[2:3, :] * lp[3:4, :], axis=1, keepdims=True))
    lam = e1 - e2 + lam_init
    o = acc_ref[...] / l_ref[...]
    od = o[:, 0:tq] - lam * o[:, tq:2 * tq]
    od = od * lax.rsqrt(jnp.mean(od * od, axis=0, keepdims=True) + EPS)
    o_ref[...] = (od.T * gd_ref[...] * (1.0 - lam_init)).astype(BF16)


def _attn_call(p_lat, vt_lat, p_ctx, vt_ctx, lamp, gd, *, batch, seq, lam_init):
    nq = seq // ATT_TQ
    n_kv = seq // ATT_TK
    kern = functools.partial(_attn_kernel, lam_init=lam_init)
    return pl.pallas_call(
        kern,
        grid=(batch, HEADS, nq),
        in_specs=[
            pl.BlockSpec((ATT_TQ, HEAD_DIM), lambda b, h, i: (b * nq + i, TILE_DQ * HEADS + h)),
            pl.BlockSpec((seq, HEAD_DIM), lambda b, h, i: (b, TILE_DK * HEADS + h)),
            pl.BlockSpec((n_kv, HEAD_DIM, ATT_TK), lambda b, h, i: (b, h, 0)),
            pl.BlockSpec((CTX_LEN, HEAD_DIM), lambda b, h, i: (b, TILE_DK * HEADS + h)),
            pl.BlockSpec((1, HEAD_DIM, CTX_LEN), lambda b, h, i: (b, h, 0)),
            pl.BlockSpec((8, LANES), lambda b, h, i: (0, 0)),
            pl.BlockSpec((1, HEAD_DIM), lambda b, h, i: (0, 0)),
        ],
        out_specs=pl.BlockSpec((ATT_TQ, HEAD_DIM), lambda b, h, i: (b * nq + i, h)),
        out_shape=jax.ShapeDtypeStruct((batch * seq, GROUP_WIDTH), BF16),
        scratch_shapes=[
            pltpu.VMEM((2 * ATT_TQ, HEAD_DIM), BF16),
            pltpu.VMEM((1, 2 * ATT_TQ), F32),
            pltpu.VMEM((1, 2 * ATT_TQ), F32),
            pltpu.VMEM((HEAD_DIM, 2 * ATT_TQ), F32),
        ],
        compiler_params=_cparams(("parallel", "parallel", "arbitrary")),
        name="diffattn",
    )(p_lat, p_lat, vt_lat, p_ctx, vt_ctx, lamp, gd)


def _outproj_kernel(m_ref, d_ref, wa_ref, wb_ref, x_ref, ga_ref, sh_ref, sc_ref, gpost_ref, gpre_ref,
                    x1_ref, h_ref):
    y = _dot(m_ref[...], wa_ref[...]) + _dot(d_ref[...], wb_ref[...])
    x1 = x_ref[...] + ga_ref[0] * _rms(y, gpost_ref[...])
    x1_ref[...] = x1
    hh = _rms(x1, gpre_ref[...]) * (1.0 + sc_ref[0]) + sh_ref[0]
    h_ref[...] = hh.astype(BF16)


def _outproj_call(m_lat, d_lat, wa, wb, x2d, ga1, sh2, sc2, gpost, gpre, *, tm, seq):
    m = x2d.shape[0]
    bpb = seq // tm
    row = lambda i: (i, 0)
    fixed = lambda i: (0, 0)
    per_batch = lambda i: (i // bpb, 0, 0)
    return pl.pallas_call(
        _outproj_kernel,
        grid=(m // tm,),
        in_specs=[
            pl.BlockSpec((tm, GROUP_WIDTH), row),
            pl.BlockSpec((tm, GROUP_WIDTH), row),
            pl.BlockSpec((GROUP_WIDTH, D_MODEL), fixed),
            pl.BlockSpec((GROUP_WIDTH, D_MODEL), fixed),
            pl.BlockSpec((tm, D_MODEL), row),
            pl.BlockSpec((1, 1, D_MODEL), per_batch),
            pl.BlockSpec((1, 1, D_MODEL), per_batch),
            pl.BlockSpec((1, 1, D_MODEL), per_batch),
            pl.BlockSpec((1, D_MODEL), fixed),
            pl.BlockSpec((1, D_MODEL), fixed),
        ],
        out_specs=[pl.BlockSpec((tm, D_MODEL), row), pl.BlockSpec((tm, D_MODEL), row)],
        out_shape=[jax.ShapeDtypeStruct((m, D_MODEL), F32), jax.ShapeDtypeStruct((m, D_MODEL), BF16)],
        compiler_params=_cparams(("parallel",)),
        name="outproj",
    )(m_lat, d_lat, wa, wb, x2d, ga1, sh2, sc2, gpost, gpre)


def _ffn_kernel(h_ref, hp_ref, hn_ref, wg_ref, wv_ref, cwg_ref, cwv_ref, cbg_ref, cbv_ref, wd_ref,
                x1_ref, ga_ref, gpost_ref, o_ref, lhs_ref, acc_ref, *, blocks_per_seq):
    i = pl.program_id(0)
    f = pl.program_id(1)
    tm = h_ref.shape[0]
    halo = BF16_SUBLANES

    @pl.when(f == 0)
    def _():
        lhs_ref[0:tm, :] = h_ref[...]
        lhs_ref[tm:tm + halo, :] = hp_ref[...]
        lhs_ref[tm + halo:tm + 2 * halo, :] = hn_ref[...]
        acc_ref[...] = jnp.zeros(acc_ref.shape, F32)

    has_prev = ((i % blocks_per_seq) > 0).astype(F32)
    has_next = ((i % blocks_per_seq) < blocks_per_seq - 1).astype(F32)
    row = lax.broadcasted_iota(jnp.int32, (tm, 1), 0)

    def conv(w_ref, cw_ref, cb_ref):
        u = _dot(lhs_ref[...], w_ref[...])
        main = u[0:tm, :]
        prev_row = u[tm + halo - 1:tm + halo, :] * has_prev
        next_row = u[tm + halo:tm + halo + 1, :] * has_next
        up = jnp.where(row == 0, prev_row, pltpu.roll(main, 1, 0))
        dn = jnp.where(row == tm - 1, next_row, pltpu.roll(main, tm - 1, 0))
        cw = cw_ref[...]
        return up * cw[0:1, :] + main * cw[1:2, :] + dn * cw[2:3, :] + cb_ref[...]

    gate = conv(wg_ref, cwg_ref, cbg_ref)
    val = conv(wv_ref, cwv_ref, cbv_ref)
    a = (gate * jax.nn.sigmoid(gate) * val).astype(BF16)
    acc_ref[...] += _dot(a, wd_ref[...])

    @pl.when(f == pl.num_programs(1) - 1)
    def _():
        o_ref[...] = x1_ref[...] + ga_ref[0] * _rms(acc_ref[...], gpost_ref[...])


def _ffn_call(hff, w_up, cw, cb, w_down, x1, ga2, gpost, *, tm, tf, seq):
    m = hff.shape[0]
    nf = D_FF // tf
    bps = seq // tm
    halo = BF16_SUBLANES
    hb = tm // halo
    n_halo = m // halo
    kern = functools.partial(_ffn_kernel, blocks_per_seq=bps)
    row = lambda i, f: (i, 0)
    fixed = lambda i, f: (0, 0)
    return pl.pallas_call(
        kern,
        grid=(m // tm, nf),
        in_specs=[
            pl.BlockSpec((tm, D_MODEL), row),
            pl.BlockSpec((halo, D_MODEL), lambda i, f: (jnp.maximum(i * hb - 1, 0), 0)),
            pl.BlockSpec((halo, D_MODEL), lambda i, f: (jnp.minimum((i + 1) * hb, n_halo - 1), 0)),
            pl.BlockSpec((D_MODEL, tf), lambda i, f: (0, f)),
            pl.BlockSpec((D_MODEL, tf), lambda i, f: (0, nf + f)),
            pl.BlockSpec((3, tf), lambda i, f: (0, f)),
            pl.BlockSpec((3, tf), lambda i, f: (0, nf + f)),
            pl.BlockSpec((1, tf), lambda i, f: (0, f)),
            pl.BlockSpec((1, tf), lambda i, f: (0, nf + f)),
            pl.BlockSpec((tf, D_MODEL), lambda i, f: (f, 0)),
            pl.BlockSpec((tm, D_MODEL), row),
            pl.BlockSpec((1, 1, D_MODEL), lambda i, f: (i // bps, 0, 0)),
            pl.BlockSpec((1, D_MODEL), fixed),
        ],
        out_specs=pl.BlockSpec((tm, D_MODEL), row),
        out_shape=jax.ShapeDtypeStruct((m, D_MODEL), F32),
        scratch_shapes=[pltpu.VMEM((tm + 2 * halo, D_MODEL), BF16), pltpu.VMEM((tm, D_MODEL), F32)],
        compiler_params=_cparams(("parallel", "arbitrary")),
        name="convffn",
    )(hff, hff, hff, w_up, w_up, cw, cw, cb, cb, w_down, x1, ga2, gpost)


def _rope_tables(seq):
    rows = seq // GRID_W
    r = jnp.repeat(jnp.arange(rows, dtype=F32), GRID_W)
    c = jnp.tile(jnp.arange(GRID_W, dtype=F32), rows)
    inv_freq = ROPE_BASE ** (-jnp.arange(ROPE_FREQS, dtype=F32) / ROPE_FREQS)
    ar = r[:, None] * inv_freq
    ac = c[:, None] * inv_freq
    ang = jnp.concatenate([ar, ar, ac, ac, ar, ar, ac, ac], axis=1)
    sign = jnp.tile(jnp.concatenate([-jnp.ones(ROPE_FREQS, F32), jnp.ones(ROPE_FREQS, F32)]), 4)
    return jnp.cos(ang), jnp.sin(ang) * sign


def kernel(x, c, ctx, c_ctx, w_mod, b_mod, g_pre_mix, g_post_mix, w_in, b_gate, conv_qk_w, conv_qk_b,
           g_mlstm, lambda_q1, lambda_k1, lambda_q2, lambda_k2, g_diff, w_out, g_pre_ffn, g_post_ffn,
           w_up, conv_ffn_w, conv_ffn_b, w_down):
    batch, seq, d = x.shape
    depth = w_mod.shape[0]
    assert depth == 1 and d == D_MODEL and ctx.shape[1] == CTX_LEN
    l = 0
    lam_init = 0.8 - 0.6 * math.exp(-0.3 * l)

    cc = jnp.zeros((8, d), F32).at[:batch].set(c).at[batch].set(c_ctx)
    mod = _mod_call(cc, w_mod[l], b_mod[l][None, :])
    mod = mod.reshape(8, N_MOD, 1, d)
    sh1, sc1, ga1, sh2, sc2, ga2 = (mod[:batch, k] for k in range(N_MOD))
    csh1 = jnp.broadcast_to(mod[batch, 0][None], (batch, 1, d))
    csc1 = jnp.broadcast_to(mod[batch, 1][None], (batch, 1, d))

    wl = w_in[l]
    wp = jnp.concatenate([wl[:, :OFF_MG], wl[:, OFF_DQ:]], axis=1).astype(BF16)
    wgt = wl[:, OFF_MG:OFF_DQ].T.astype(BF16)

    cos_t, sin_t = _rope_tables(seq)
    g_pre = g_pre_mix[l][None, :]
    x2d = x.reshape(batch * seq, d)
    ctx2d = ctx.reshape(batch * CTX_LEN, d)

    p_lat, vt_lat, gt_lat = _inproj_call(x2d, g_pre, sh1, sc1, wp, wgt, cos_t, sin_t,
                                         tm=512, rows_per_batch=seq, rope=True, tk_v=ATT_TK)
    p_ctx, vt_ctx, gt_ctx = _inproj_call(ctx2d, g_pre, csh1, csc1, wp, wgt, cos_t, sin_t,
                                         tm=CTX_LEN, rows_per_batch=CTX_LEN, rope=False, tk_v=CTX_LEN)

    m_lat = _mlstm_call(b_gate[l], p_lat, gt_lat, p_ctx, gt_ctx, conv_qk_w[l], conv_qk_b[l][None, :],
                        g_mlstm[l][None, :], batch=batch, seq=seq)

    lamp = jnp.zeros((8, LANES), F32)
    lamp = lamp.at[0, :DIFF_QK_DIM].set(lambda_q1[l]).at[1, :DIFF_QK_DIM].set(lambda_k1[l])
    lamp = lamp.at[2, :DIFF_QK_DIM].set(lambda_q2[l]).at[3, :DIFF_QK_DIM].set(lambda_k2[l])
    d_lat = _attn_call(p_lat, vt_lat, p_ctx, vt_ctx, lamp, g_diff[l][None, :],
                       batch=batch, seq=seq, lam_init=lam_init)

    wo = w_out[l].astype(BF16)
    x1, hff = _outproj_call(m_lat, d_lat, wo[:GROUP_WIDTH], wo[GROUP_WIDTH:], x2d, ga1, sh2, sc2,
                            g_post_mix[l][None, :], g_pre_ffn[l][None, :], tm=512, seq=seq)

    x2 = _ffn_call(hff, w_up[l].astype(BF16), conv_ffn_w[l], conv_ffn_b[l][None, :],
                   w_down[l].astype(BF16), x1, ga2, g_post_ffn[l][None, :], tm=512, tf=512, seq=seq)
    return x2.reshape(batch, seq, d)
```

```python
import functools
import math

import jax
import jax.numpy as jnp
from jax import lax
from jax.experimental import pallas as pl
from jax.experimental.pallas import tpu as pltpu

F32 = jnp.float32
BF16 = jnp.bfloat16

D_MODEL = 2048
CTX_LEN = 256
GRID_W = 64
HEADS = 8
HEAD_DIM = 128
GROUP_WIDTH = HEADS * HEAD_DIM
CHUNK = 128
DIFF_QK_DIM = 64
ROPE_FREQS = 16
ROPE_BASE = 10000.0
D_FF = 5632
N_MOD = 6
EPS = 1e-6
N_GATES = 4 * HEADS

OFF_MG = 4 * GROUP_WIDTH
OFF_DQ = OFF_MG + N_GATES

LANES = 128
BF16_SUBLANES = 16
VMEM_LIMIT = 56 * 1024 * 1024

TILE_MQ, TILE_MK, TILE_MV, TILE_MO, TILE_DQ, TILE_DK, TILE_DV = range(7)
N_PROJ_TILES = 7
P_TILES = 6

ATT_TQ = 512
ATT_TK = 256
ATT_STRIP = 256
QK_SCALE_LOG2E = DIFF_QK_DIM ** -0.5 * 1.4426950408889634


def _cparams(sem):
    return pltpu.CompilerParams(dimension_semantics=sem, vmem_limit_bytes=VMEM_LIMIT)


def _dot(a, b):
    return jnp.dot(a, b, preferred_element_type=F32)


def _dot_nt(a, b):
    return lax.dot_general(a, b, (((1,), (1,)), ((), ())), preferred_element_type=F32)


def _rms(y, g):
    return y * lax.rsqrt(jnp.mean(y * y, axis=-1, keepdims=True) + EPS) * g


def _mod_kernel(c_ref, w_ref, b_ref, o_ref):
    cc = c_ref[...]
    s = cc * jax.nn.sigmoid(cc)
    o_ref[...] = _dot(s.astype(BF16), w_ref[...].astype(BF16)) + b_ref[...]


def _mod_call(cc, w_mod, b_mod):
    n = w_mod.shape[1]
    tn = 1024
    return pl.pallas_call(
        _mod_kernel,
        grid=(n // tn,),
        in_specs=[pl.BlockSpec((8, D_MODEL), lambda j: (0, 0)),
                  pl.BlockSpec((D_MODEL, tn), lambda j: (0, j)),
                  pl.BlockSpec((1, tn), lambda j: (0, j))],
        out_specs=pl.BlockSpec((8, tn), lambda j: (0, j)),
        out_shape=jax.ShapeDtypeStruct((8, n), F32),
        compiler_params=_cparams(("parallel",)),
        name="mod",
    )(cc, w_mod, b_mod)


def _rope(acc, cos, sin_signed, first_half):
    out = []
    for c in range(acc.shape[1] // LANES):
        a = acc[:, c * LANES:(c + 1) * LANES]
        rot = jnp.where(first_half, pltpu.roll(a, LANES - ROPE_FREQS, 1), pltpu.roll(a, ROPE_FREQS, 1))
        out.append(a * cos + rot * sin_signed)
    return jnp.concatenate(out, axis=1)


def _inproj_kernel(x_ref, g_ref, sh_ref, sc_ref, w_ref, wgt_ref, cos_ref, sin_ref,
                   p_ref, vt_ref, gt_ref, xn_ref, *, rope, tk_v):
    j = pl.program_id(1)
    tm = x_ref.shape[0]

    @pl.when(j == 0)
    def _():
        x = x_ref[...]
        y = _rms(x, g_ref[...]) * (1.0 + sc_ref[0]) + sh_ref[0]
        xn = y.astype(BF16)
        xn_ref[...] = xn
        gt = _dot_nt(wgt_ref[...], xn)
        for c in range(tm // CHUNK):
            gt_ref[c] = gt[:, c * CHUNK:(c + 1) * CHUNK]

    acc = _dot(xn_ref[...], w_ref[...])

    is_rope = (j == TILE_DQ) | (j == TILE_DK)
    is_v = j == TILE_DV

    if rope:
        @pl.when(is_rope)
        def _():
            lane = lax.broadcasted_iota(jnp.int32, (tm, LANES), 1)
            first_half = (lane % (2 * ROPE_FREQS)) < ROPE_FREQS
            fac = jnp.where(j == TILE_DQ, QK_SCALE_LOG2E, 1.0)
            p_ref[...] = (_rope(acc, cos_ref[...], sin_ref[...], first_half) * fac).astype(BF16)

        plain = jnp.logical_not(is_rope | is_v)
    else:
        plain = jnp.logical_not(is_v)

    @pl.when(plain)
    def _():
        p_ref[...] = acc.astype(BF16)

    @pl.when(is_v)
    def _():
        at = acc.T.astype(BF16)
        for c in range(tm // tk_v):
            vt_ref[c] = at[:, c * tk_v:(c + 1) * tk_v]


def _inproj_call(x2d, g, sh, sc, wp, wgt, cos_t, sin_t, *, tm, rows_per_batch, rope, tk_v):
    m = x2d.shape[0]
    blocks_per_batch = rows_per_batch // tm
    kern = functools.partial(_inproj_kernel, rope=rope, tk_v=tk_v)
    return pl.pallas_call(
        kern,
        grid=(m // tm, N_PROJ_TILES),
        in_specs=[
            pl.BlockSpec((tm, D_MODEL), lambda i, j: (i, 0)),
            pl.BlockSpec((1, D_MODEL), lambda i, j: (0, 0)),
            pl.BlockSpec((1, 1, D_MODEL), lambda i, j: (i // blocks_per_batch, 0, 0)),
            pl.BlockSpec((1, 1, D_MODEL), lambda i, j: (i // blocks_per_batch, 0, 0)),
            pl.BlockSpec((D_MODEL, GROUP_WIDTH), lambda i, j: (0, j)),
            pl.BlockSpec((N_GATES, D_MODEL), lambda i, j: (0, 0)),
            pl.BlockSpec((tm, LANES), lambda i, j: (i % blocks_per_batch, 0)),
            pl.BlockSpec((tm, LANES), lambda i, j: (i % blocks_per_batch, 0)),
        ],
        out_specs=[
            pl.BlockSpec((tm, GROUP_WIDTH), lambda i, j: (i, jnp.minimum(j, P_TILES - 1))),
            pl.BlockSpec((tm // tk_v, GROUP_WIDTH, tk_v), lambda i, j: (i, 0, 0)),
            pl.BlockSpec((tm // CHUNK, N_GATES, CHUNK), lambda i, j: (i, 0, 0)),
        ],
        out_shape=[
            jax.ShapeDtypeStruct((m, P_TILES * GROUP_WIDTH), BF16),
            jax.ShapeDtypeStruct((m // tk_v, GROUP_WIDTH, tk_v), BF16),
            jax.ShapeDtypeStruct((m // CHUNK, N_GATES, CHUNK), F32),
        ],
        scratch_shapes=[pltpu.VMEM((tm, D_MODEL), BF16)],
        compiler_params=_cparams(("parallel", "arbitrary")),
        name="inproj_rope" if rope else "inproj_ctx",
    )(x2d, g, sh, sc, wp, wgt, cos_t, sin_t)


def _log_sigmoid(x):
    return jnp.minimum(x, 0.0) - jnp.log1p(jnp.exp(-jnp.abs(x)))


def _split_dot(row, tri):
    hi = row.astype(BF16)
    lo = (row - hi.astype(F32)).astype(BF16)
    hi8 = jnp.broadcast_to(hi, (BF16_SUBLANES, CHUNK))
    lo8 = jnp.broadcast_to(lo, (BF16_SUBLANES, CHUNK))
    return (_dot(hi8, tri) + _dot(lo8, tri))[0:1, :]


def _mlstm_kernel(bg_ref,
                  q_ref, k_ref, v_ref, o_ref, gt_ref,
                  kc_ref, vc_ref, gtc_ref,
                  cwq_ref, cwk_ref, cbq_ref, cbk_ref, gn_ref,
                  out_ref,
                  qs_ref, kst_ref, gs_ref, kstc_ref, gsc_ref, hf_ref, hb_ref, cst_ref):
    h = pl.program_id(1)
    n_lat = q_ref.shape[0] // CHUNK
    n_ctx = kc_ref.shape[0] // CHUNK

    row_i = lax.broadcasted_iota(jnp.int32, (CHUNK, CHUNK), 0)
    col_i = lax.broadcasted_iota(jnp.int32, (CHUNK, CHUNK), 1)
    tril = col_i <= row_i
    triu = col_i >= row_i
    tril_b = tril.astype(BF16)
    triu_b = triu.astype(BF16)
    ones_col = (col_i == 0).astype(BF16)
    first_row = row_i == 0
    last_row = row_i == CHUNK - 1

    def conv_silu(ref, c, n_chunks, w_ref, b_ref):
        r0 = pl.multiple_of(c * CHUNK, CHUNK)
        main = ref[pl.ds(r0, CHUNK), :].astype(F32)
        pstart = pl.multiple_of(jnp.maximum(r0 - BF16_SUBLANES, 0), BF16_SUBLANES)
        nstart = pl.multiple_of(jnp.minimum(r0 + CHUNK, (n_chunks - 1) * CHUNK), BF16_SUBLANES)
        prev_blk = ref[pl.ds(pstart, BF16_SUBLANES), :].astype(F32)
        next_blk = ref[pl.ds(nstart, BF16_SUBLANES), :].astype(F32)
        prev_row = prev_blk[BF16_SUBLANES - 1:BF16_SUBLANES, :] * jnp.where(c > 0, 1.0, 0.0)
        next_row = next_blk[0:1, :] * jnp.where(c < n_chunks - 1, 1.0, 0.0)
        up = jnp.where(first_row, prev_row, pltpu.roll(main, 1, 0))
        dn = jnp.where(last_row, next_row, pltpu.roll(main, CHUNK - 1, 0))
        w = w_ref[...]
        u = up * w[0:1, :] + main * w[1:2, :] + dn * w[2:3, :] + b_ref[...]
        return u * jax.nn.sigmoid(u)

    def gate_rows(g_ref3, dst_ref, c):
        def row(t):
            return g_ref3[c, pl.ds(t * HEADS + h, 1), :] + bg_ref[t, h]
        i_f, f_f, i_b, f_b = row(0), row(1), row(2), row(3)
        lf_f = _log_sigmoid(f_f)
        lf_b = _log_sigmoid(f_b)
        dst_ref[c, 0:1, :] = i_f - _split_dot(lf_f, triu_b)
        dst_ref[c, 1:2, :] = lf_f
        dst_ref[c, 2:3, :] = i_b - _split_dot(lf_b, tril_b)
        dst_ref[c, 3:4, :] = lf_b

    k_scale = HEAD_DIM ** -0.5

    def prep_lat(c, carry):
        qs_ref[c] = conv_silu(q_ref, c, n_lat, cwq_ref, cbq_ref).astype(BF16)
        kk = conv_silu(k_ref, c, n_lat, cwk_ref, cbk_ref) * k_scale
        kst_ref[c] = kk.T.astype(BF16)
        gate_rows(gt_ref, gs_ref, c)
        return carry

    lax.fori_loop(0, n_lat, prep_lat, 0)

    def prep_ctx(c, carry):
        kk = conv_silu(kc_ref, c, n_ctx, cwk_ref, cbk_ref) * k_scale
        kstc_ref[c] = kk.T.astype(BF16)
        gate_rows(gtc_ref, gsc_ref, c)
        return carry

    lax.fori_loop(0, n_ctx, prep_ctx, 0)

    def vaug(vref, c):
        r0 = pl.multiple_of(c * CHUNK, CHUNK)
        return jnp.concatenate([vref[pl.ds(r0, CHUNK), :], ones_col], axis=1)

    def state_update(d, m_prev, a_row, lf_row, kt, va):
        m_last = jnp.maximum(m_prev, jnp.max(a_row, axis=1, keepdims=True))
        b_last = jnp.sum(lf_row, axis=1, keepdims=True)
        w_row = jnp.exp(a_row - m_last)
        decay = jnp.exp(m_prev - m_last)
        kw = (kt.astype(F32) * w_row).astype(BF16)
        cst_ref[d] = decay * cst_ref[d] + _dot(kw, va)
        return b_last + m_last

    def chunk_step(d, c, m_prev, mask):
        a_row = gs_ref[c, 2 * d:2 * d + 1, :]
        lf_row = gs_ref[c, 2 * d + 1:2 * d + 2, :]
        q = qs_ref[c]
        kt = kst_ref[c]
        va = vaug(v_ref, c)
        a_mat = jnp.where(mask, a_row, -jnp.inf)
        m_col = jnp.maximum(jnp.max(a_mat, axis=1, keepdims=True), m_prev)
        w_mat = jnp.exp(a_mat - m_col)
        w_inter = jnp.exp(m_prev - m_col)
        b_col = jnp.sum(jnp.where(mask, lf_row, 0.0), axis=1, keepdims=True)
        s = (_dot(q, kt) * w_mat).astype(BF16)
        nd = w_inter * _dot(q, cst_ref[d].astype(BF16)) + _dot(s, va)
        num = nd[:, :HEAD_DIM]
        den = nd[:, HEAD_DIM:HEAD_DIM + 1]
        hh = num / jnp.maximum(jnp.abs(den), jnp.exp(-b_col - m_col))
        m_new = state_update(d, m_prev, a_row, lf_row, kt, va)
        return hh, m_new

    cst_ref[...] = jnp.zeros(cst_ref.shape, F32)

    def ctx_fwd(c, m):
        return state_update(0, m, gsc_ref[c, 0:1, :], gsc_ref[c, 1:2, :], kstc_ref[c], vaug(vc_ref, c))

    def ctx_bwd(i, m):
        c = n_ctx - 1 - i
        return state_update(1, m, gsc_ref[c, 2:3, :], gsc_ref[c, 3:4, :], kstc_ref[c], vaug(vc_ref, c))

    m_f = lax.fori_loop(0, n_ctx, ctx_fwd, jnp.zeros((1, 1), F32))
    m_b = lax.fori_loop(0, n_ctx, ctx_bwd, jnp.zeros((1, 1), F32))

    def main_step(i, carry):
        m_f, m_b = carry
        hf, m_f = chunk_step(0, i, m_f, tril)
        hf_ref[i] = hf
        cb = n_lat - 1 - i
        hb, m_b = chunk_step(1, cb, m_b, triu)
        hb_ref[cb] = hb
        return m_f, m_b

    lax.fori_loop(0, n_lat, main_step, (m_f, m_b))

    def finish(c, carry):
        r0 = pl.multiple_of(c * CHUNK, CHUNK)
        hh = _rms(hf_ref[c] + hb_ref[c], gn_ref[...])
        og = o_ref[pl.ds(r0, CHUNK), :].astype(F32)
        out_ref[pl.ds(r0, CHUNK), :] = (hh * jax.nn.sigmoid(og)).astype(BF16)
        return carry

    lax.fori_loop(0, n_lat, finish, 0)


def _mlstm_call(bg, p_lat, gt_lat, p_ctx, gt_ctx, cw, cb, gn, *, batch, seq):
    n_lat = seq // CHUNK
    n_ctx = CTX_LEN // CHUNK

    def col(tile):
        return lambda b, h: (b, tile * HEADS + h)

    lat_blk = (seq, HEAD_DIM)
    ctx_blk = (CTX_LEN, HEAD_DIM)
    return pl.pallas_call(
        _mlstm_kernel,
        grid=(batch, HEADS),
        in_specs=[
            pl.BlockSpec(memory_space=pltpu.SMEM),
            pl.BlockSpec(lat_blk, col(TILE_MQ)),
            pl.BlockSpec(lat_blk, col(TILE_MK)),
            pl.BlockSpec(lat_blk, col(TILE_MV)),
            pl.BlockSpec(lat_blk, col(TILE_MO)),
            pl.BlockSpec((n_lat, N_GATES, CHUNK), lambda b, h: (b, 0, 0)),
            pl.BlockSpec(ctx_blk, col(TILE_MK)),
            pl.BlockSpec(ctx_blk, col(TILE_MV)),
            pl.BlockSpec((n_ctx, N_GATES, CHUNK), lambda b, h: (b, 0, 0)),
            pl.BlockSpec((3, HEAD_DIM), lambda b, h: (0, h)),
            pl.BlockSpec((3, HEAD_DIM), lambda b, h: (0, HEADS + h)),
            pl.BlockSpec((1, HEAD_DIM), lambda b, h: (0, h)),
            pl.BlockSpec((1, HEAD_DIM), lambda b, h: (0, HEADS + h)),
            pl.BlockSpec((1, HEAD_DIM), lambda b, h: (0, h)),
        ],
        out_specs=pl.BlockSpec(lat_blk, lambda b, h: (b, h)),
        out_shape=jax.ShapeDtypeStruct((batch * seq, GROUP_WIDTH), BF16),
        scratch_shapes=[
            pltpu.VMEM((n_lat, CHUNK, HEAD_DIM), BF16),
            pltpu.VMEM((n_lat, HEAD_DIM, CHUNK), BF16),
            pltpu.VMEM((n_lat, 8, CHUNK), F32),
            pltpu.VMEM((n_ctx, HEAD_DIM, CHUNK), BF16),
            pltpu.VMEM((n_ctx, 8, CHUNK), F32),
            pltpu.VMEM((n_lat, CHUNK, HEAD_DIM), F32),
            pltpu.VMEM((n_lat, CHUNK, HEAD_DIM), F32),
            pltpu.VMEM((2, HEAD_DIM, 2 * HEAD_DIM), F32),
        ],
        compiler_params=_cparams(("parallel", "parallel")),
        name="mlstm",
    )(bg, p_lat, p_lat, p_lat, p_lat, gt_lat, p_ctx, p_ctx, gt_ctx, cw, cw, cb, cb, gn)


def _attn_kernel(q_ref, k_ref, vt_ref, kc_ref, vtc_ref, lam_ref, gd_ref, o_ref,
                 qq_ref, m_ref, l_ref, acc_ref, s0_ref, s1_ref, cm0_ref, cm1_ref, *, lam_init):
    tq = q_ref.shape[0]
    n_kv = k_ref.shape[0] // ATT_TK

    lane = lax.broadcasted_iota(jnp.int32, (tq, HEAD_DIM), 1)
    q = q_ref[...]
    zero = jnp.zeros_like(q)
    qq_ref[0:tq, :] = jnp.where(lane < DIFF_QK_DIM, q, zero)
    qq_ref[tq:2 * tq, :] = jnp.where(lane >= DIFF_QK_DIM, q, zero)
    m_ref[...] = jnp.full(m_ref.shape, -jnp.inf, F32)
    l_ref[...] = jnp.zeros(l_ref.shape, F32)
    acc_ref[...] = jnp.zeros(acc_ref.shape, F32)

    bufs = ((s0_ref, cm0_ref), (s1_ref, cm1_ref))

    def score(k, buf):
        s_ref, cm_ref = bufs[buf]
        st = _dot_nt(k, qq_ref[...])
        s_ref[...] = st
        cm_ref[...] = jnp.max(st, axis=0, keepdims=True)

    def accumulate(vt, buf):
        s_ref, cm_ref = bufs[buf]
        m_prev = m_ref[...]
        m_new = jnp.maximum(m_prev, cm_ref[...])
        p = jnp.exp2(s_ref[...] - m_new)
        alpha = jnp.exp2(m_prev - m_new)
        l_ref[...] = alpha * l_ref[...] + jnp.sum(p, axis=0, keepdims=True)
        acc_ref[...] = alpha * acc_ref[...] + _dot(vt, p.astype(BF16))
        m_ref[...] = m_new

    def k_tile(t):
        return k_ref[pl.ds(pl.multiple_of(t * ATT_TK, ATT_TK), ATT_TK), :]

    score(kc_ref[...], 0)
    score(k_tile(0), 1)
    accumulate(vtc_ref[0], 0)

    def body(i, carry):
        t = 2 * i
        score(k_tile(t + 1), 0)
        accumulate(vt_ref[t], 1)
        score(k_tile(jnp.minimum(t + 2, n_kv - 1)), 1)
        accumulate(vt_ref[t + 1], 0)
        return carry

    lax.fori_loop(0, n_kv // 2, body, 0)

    lp = lam_ref[...]
    e1 = jnp.exp(jnp.sum(lp[0:1, :] * lp[1:2, :], axis=1, keepdims=True))
    e2 = jnp.exp(jnp.sum(lp[2:3, :] * lp[3:4, :], axis=1, keepdims=True))
    lam = e1 - e2 + lam_init
    o = acc_ref[...] / l_ref[...]
    od = o[:, 0:tq] - lam * o[:, tq:2 * tq]
    od = od * lax.rsqrt(jnp.mean(od * od, axis=0, keepdims=True) + EPS)
    o_ref[...] = (od.T * gd_ref[...] * (1.0 - lam_init)).astype(BF16)


def _attn_call(p_lat, vt_lat, p_ctx, vt_ctx, lamp, gd, *, batch, seq, lam_init):
    nq = seq // ATT_TQ
    n_kv = seq // ATT_TK
    kern = functools.partial(_attn_kernel, lam_init=lam_init)
    return pl.pallas_call(
        kern,
        grid=(batch, HEADS, nq),
        in_specs=[
            pl.BlockSpec((ATT_TQ, HEAD_DIM), lambda b, h, i: (b * nq + i, TILE_DQ * HEADS + h)),
            pl.BlockSpec((seq, HEAD_DIM), lambda b, h, i: (b, TILE_DK * HEADS + h)),
            pl.BlockSpec((n_kv, HEAD_DIM, ATT_TK), lambda b, h, i: (b, h, 0)),
            pl.BlockSpec((CTX_LEN, HEAD_DIM), lambda b, h, i: (b, TILE_DK * HEADS + h)),
            pl.BlockSpec((1, HEAD_DIM, CTX_LEN), lambda b, h, i: (b, h, 0)),
            pl.BlockSpec((8, LANES), lambda b, h, i: (0, 0)),
            pl.BlockSpec((1, HEAD_DIM), lambda b, h, i: (0, 0)),
        ],
        out_specs=pl.BlockSpec((ATT_TQ, HEAD_DIM), lambda b, h, i: (b * nq + i, h)),
        out_shape=jax.ShapeDtypeStruct((batch * seq, GROUP_WIDTH), BF16),
        scratch_shapes=[
            pltpu.VMEM((2 * ATT_TQ, HEAD_DIM), BF16),
            pltpu.VMEM((1, 2 * ATT_TQ), F32),
            pltpu.VMEM((1, 2 * ATT_TQ), F32),
            pltpu.VMEM((HEAD_DIM, 2 * ATT_TQ), F32),
            pltpu.VMEM((ATT_TK, 2 * ATT_TQ), F32),
            pltpu.VMEM((ATT_TK, 2 * ATT_TQ), F32),
            pltpu.VMEM((1, 2 * ATT_TQ), F32),
            pltpu.VMEM((1, 2 * ATT_TQ), F32),
        ],
        compiler_params=_cparams(("parallel", "parallel", "arbitrary")),
        name="diffattn",
    )(p_lat, p_lat, vt_lat, p_ctx, vt_ctx, lamp, gd)


def _outproj_kernel(m_ref, d_ref, wa_ref, wb_ref, x_ref, ga_ref, sh_ref, sc_ref, gpost_ref, gpre_ref,
                    x1_ref, h_ref):
    y = _dot(m_ref[...], wa_ref[...]) + _dot(d_ref[...], wb_ref[...])
    x1 = x_ref[...] + ga_ref[0] * _rms(y, gpost_ref[...])
    x1_ref[...] = x1
    hh = _rms(x1, gpre_ref[...]) * (1.0 + sc_ref[0]) + sh_ref[0]
    h_ref[...] = hh.astype(BF16)


def _outproj_call(m_lat, d_lat, wa, wb, x2d, ga1, sh2, sc2, gpost, gpre, *, tm, seq):
    m = x2d.shape[0]
    bpb = seq // tm
    row = lambda i: (i, 0)
    fixed = lambda i: (0, 0)
    per_batch = lambda i: (i // bpb, 0, 0)
    return pl.pallas_call(
        _outproj_kernel,
        grid=(m // tm,),
        in_specs=[
            pl.BlockSpec((tm, GROUP_WIDTH), row),
            pl.BlockSpec((tm, GROUP_WIDTH), row),
            pl.BlockSpec((GROUP_WIDTH, D_MODEL), fixed),
            pl.BlockSpec((GROUP_WIDTH, D_MODEL), fixed),
            pl.BlockSpec((tm, D_MODEL), row),
            pl.BlockSpec((1, 1, D_MODEL), per_batch),
            pl.BlockSpec((1, 1, D_MODEL), per_batch),
            pl.BlockSpec((1, 1, D_MODEL), per_batch),
            pl.BlockSpec((1, D_MODEL), fixed),
            pl.BlockSpec((1, D_MODEL), fixed),
        ],
        out_specs=[pl.BlockSpec((tm, D_MODEL), row), pl.BlockSpec((tm, D_MODEL), row)],
        out_shape=[jax.ShapeDtypeStruct((m, D_MODEL), F32), jax.ShapeDtypeStruct((m, D_MODEL), BF16)],
        compiler_params=_cparams(("parallel",)),
        name="outproj",
    )(m_lat, d_lat, wa, wb, x2d, ga1, sh2, sc2, gpost, gpre)


def _ffn_kernel(h_ref, hp_ref, hn_ref, wg_ref, wv_ref, cwg_ref, cwv_ref, cbg_ref, cbv_ref, wd_ref,
                x1_ref, ga_ref, gpost_ref, o_ref, lhs_ref, acc_ref, *, blocks_per_seq):
    i = pl.program_id(0)
    f = pl.program_id(1)
    tm = h_ref.shape[0]
    halo = BF16_SUBLANES

    @pl.when(f == 0)
    def _():
        lhs_ref[0:tm, :] = h_ref[...]
        lhs_ref[tm:tm + halo, :] = hp_ref[...]
        lhs_ref[tm + halo:tm + 2 * halo, :] = hn_ref[...]
        acc_ref[...] = jnp.zeros(acc_ref.shape, F32)

    has_prev = ((i % blocks_per_seq) > 0).astype(F32)
    has_next = ((i % blocks_per_seq) < blocks_per_seq - 1).astype(F32)
    row = lax.broadcasted_iota(jnp.int32, (tm, 1), 0)

    def conv(w_ref, cw_ref, cb_ref):
        u = _dot(lhs_ref[...], w_ref[...])
        main = u[0:tm, :]
        prev_row = u[tm + halo - 1:tm + halo, :] * has_prev
        next_row = u[tm + halo:tm + halo + 1, :] * has_next
        up = jnp.where(row == 0, prev_row, pltpu.roll(main, 1, 0))
        dn = jnp.where(row == tm - 1, next_row, pltpu.roll(main, tm - 1, 0))
        cw = cw_ref[...]
        return up * cw[0:1, :] + main * cw[1:2, :] + dn * cw[2:3, :] + cb_ref[...]

    gate = conv(wg_ref, cwg_ref, cbg_ref)
    val = conv(wv_ref, cwv_ref, cbv_ref)
    a = (gate * jax.nn.sigmoid(gate) * val).astype(BF16)
    acc_ref[...] += _dot(a, wd_ref[...])

    @pl.when(f == pl.num_programs(1) - 1)
    def _():
        o_ref[...] = x1_ref[...] + ga_ref[0] * _rms(acc_ref[...], gpost_ref[...])


def _ffn_call(hff, w_up, cw, cb, w_down, x1, ga2, gpost, *, tm, tf, seq):
    m = hff.shape[0]
    nf = D_FF // tf
    bps = seq // tm
    halo = BF16_SUBLANES
    hb = tm // halo
    n_halo = m // halo
    kern = functools.partial(_ffn_kernel, blocks_per_seq=bps)
    row = lambda i, f: (i, 0)
    fixed = lambda i, f: (0, 0)
    return pl.pallas_call(
        kern,
        grid=(m // tm, nf),
        in_specs=[
            pl.BlockSpec((tm, D_MODEL), row),
            pl.BlockSpec((halo, D_MODEL), lambda i, f: (jnp.maximum(i * hb - 1, 0), 0)),
            pl.BlockSpec((halo, D_MODEL), lambda i, f: (jnp.minimum((i + 1) * hb, n_halo - 1), 0)),
            pl.BlockSpec((D_MODEL, tf), lambda i, f: (0, f)),
            pl.BlockSpec((D_MODEL, tf), lambda i, f: (0, nf + f)),
            pl.BlockSpec((3, tf), lambda i, f: (0, f)),
            pl.BlockSpec((3, tf), lambda i, f: (0, nf + f)),
            pl.BlockSpec((1, tf), lambda i, f: (0, f)),
            pl.BlockSpec((1, tf), lambda i, f: (0, nf + f)),
            pl.BlockSpec((tf, D_MODEL), lambda i, f: (f, 0)),
            pl.BlockSpec((tm, D_MODEL), row),
            pl.BlockSpec((1, 1, D_MODEL), lambda i, f: (i // bps, 0, 0)),
            pl.BlockSpec((1, D_MODEL), fixed),
        ],
        out_specs=pl.BlockSpec((tm, D_MODEL), row),
        out_shape=jax.ShapeDtypeStruct((m, D_MODEL), F32),
        scratch_shapes=[pltpu.VMEM((tm + 2 * halo, D_MODEL), BF16), pltpu.VMEM((tm, D_MODEL), F32)],
        compiler_params=_cparams(("parallel", "arbitrary")),
        name="convffn",
    )(hff, hff, hff, w_up, w_up, cw, cw, cb, cb, w_down, x1, ga2, gpost)


def _rope_tables(seq):
    rows = seq // GRID_W
    r = jnp.repeat(jnp.arange(rows, dtype=F32), GRID_W)
    c = jnp.tile(jnp.arange(GRID_W, dtype=F32), rows)
    inv_freq = ROPE_BASE ** (-jnp.arange(ROPE_FREQS, dtype=F32) / ROPE_FREQS)
    ar = r[:, None] * inv_freq
    ac = c[:, None] * inv_freq
    ang = jnp.concatenate([ar, ar, ac, ac, ar, ar, ac, ac], axis=1)
    sign = jnp.tile(jnp.concatenate([-jnp.ones(ROPE_FREQS, F32), jnp.ones(ROPE_FREQS, F32)]), 4)
    return jnp.cos(ang), jnp.sin(ang) * sign


def kernel(x, c, ctx, c_ctx, w_mod, b_mod, g_pre_mix, g_post_mix, w_in, b_gate, conv_qk_w, conv_qk_b,
           g_mlstm, lambda_q1, lambda_k1, lambda_q2, lambda_k2, g_diff, w_out, g_pre_ffn, g_post_ffn,
           w_up, conv_ffn_w, conv_ffn_b, w_down):
    batch, seq, d = x.shape
    depth = w_mod.shape[0]
    assert depth == 1 and d == D_MODEL and ctx.shape[1] == CTX_LEN
    l = 0
    lam_init = 0.8 - 0.6 * math.exp(-0.3 * l)

    cc = jnp.zeros((8, d), F32).at[:batch].set(c).at[batch].set(c_ctx)
    mod = _mod_call(cc, w_mod[l], b_mod[l][None, :])
    mod = mod.reshape(8, N_MOD, 1, d)
    sh1, sc1, ga1, sh2, sc2, ga2 = (mod[:batch, k] for k in range(N_MOD))
    csh1 = jnp.broadcast_to(mod[batch, 0][None], (batch, 1, d))
    csc1 = jnp.broadcast_to(mod[batch, 1][None], (batch, 1, d))

    wl = w_in[l]
    wp = jnp.concatenate([wl[:, :OFF_MG], wl[:, OFF_DQ:]], axis=1).astype(BF16)
    wgt = wl[:, OFF_MG:OFF_DQ].T.astype(BF16)

    cos_t, sin_t = _rope_tables(seq)
    g_pre = g_pre_mix[l][None, :]
    x2d = x.reshape(batch * seq, d)
    ctx2d = ctx.reshape(batch * CTX_LEN, d)

    p_lat, vt_lat, gt_lat = _inproj_call(x2d, g_pre, sh1, sc1, wp, wgt, cos_t, sin_t,
                                         tm=512, rows_per_batch=seq, rope=True, tk_v=ATT_TK)
    p_ctx, vt_ctx, gt_ctx = _inproj_call(ctx2d, g_pre, csh1, csc1, wp, wgt, cos_t, sin_t,
                                         tm=CTX_LEN, rows_per_batch=CTX_LEN, rope=False, tk_v=CTX_LEN)

    m_lat = _mlstm_call(b_gate[l], p_lat, gt_lat, p_ctx, gt_ctx, conv_qk_w[l], conv_qk_b[l][None, :],
                        g_mlstm[l][None, :], batch=batch, seq=seq)

    lamp = jnp.zeros((8, LANES), F32)
    lamp = lamp.at[0, :DIFF_QK_DIM].set(lambda_q1[l]).at[1, :DIFF_QK_DIM].set(lambda_k1[l])
    lamp = lamp.at[2, :DIFF_QK_DIM].set(lambda_q2[l]).at[3, :DIFF_QK_DIM].set(lambda_k2[l])
    d_lat = _attn_call(p_lat, vt_lat, p_ctx, vt_ctx, lamp, g_diff[l][None, :],
                       batch=batch, seq=seq, lam_init=lam_init)

    wo = w_out[l].astype(BF16)
    x1, hff = _outproj_call(m_lat, d_lat, wo[:GROUP_WIDTH], wo[GROUP_WIDTH:], x2d, ga1, sh2, sc2,
                            g_post_mix[l][None, :], g_pre_ffn[l][None, :], tm=512, seq=seq)

    x2 = _ffn_call(hff, w_up[l].astype(BF16), conv_ffn_w[l], conv_ffn_b[l][None, :],
                   w_down[l].astype(BF16), x1, ga2, g_post_ffn[l][None, :], tm=512, tf=512, seq=seq)
    return x2.reshape(batch, seq, d)
```

```python
import functools
import math

import jax
import jax.numpy as jnp
from jax import lax
from jax.experimental import pallas as pl
from jax.experimental.pallas import tpu as pltpu

F32 = jnp.float32
BF16 = jnp.bfloat16

D_MODEL = 2048
CTX_LEN = 256
GRID_W = 64
HEADS = 8
HEAD_DIM = 128
GROUP_WIDTH = HEADS * HEAD_DIM
CHUNK = 128
DIFF_QK_DIM = 64
ROPE_FREQS = 16
ROPE_BASE = 10000.0
D_FF = 5632
N_MOD = 6
EPS = 1e-6
N_GATES = 4 * HEADS

OFF_MG = 4 * GROUP_WIDTH
OFF_DQ = OFF_MG + N_GATES

LANES = 128
BF16_SUBLANES = 16
VMEM_LIMIT = 56 * 1024 * 1024

TILE_MQ, TILE_MK, TILE_MV, TILE_MO, TILE_DQ, TILE_DK, TILE_DV = range(7)
N_PROJ_TILES = 7
P_TILES = 6

ATT_TQ = 1024
ATT_TK = 256
ATT_STRIP = 256
FFN_ROWS = 64
QK_SCALE_LOG2E = DIFF_QK_DIM ** -0.5 * 1.4426950408889634


def _cparams(sem):
    return pltpu.CompilerParams(dimension_semantics=sem, vmem_limit_bytes=VMEM_LIMIT)


def _dot(a, b):
    return jnp.dot(a, b, preferred_element_type=F32)


def _dot_nt(a, b):
    return lax.dot_general(a, b, (((1,), (1,)), ((), ())), preferred_element_type=F32)


def _rms(y, g):
    return y * lax.rsqrt(jnp.mean(y * y, axis=-1, keepdims=True) + EPS) * g


def _mod_kernel(c_ref, w_ref, b_ref, o_ref):
    cc = c_ref[...]
    s = cc * jax.nn.sigmoid(cc)
    o_ref[...] = _dot(s.astype(BF16), w_ref[...].astype(BF16)) + b_ref[...]


def _mod_call(cc, w_mod, b_mod):
    n = w_mod.shape[1]
    tn = 1024
    return pl.pallas_call(
        _mod_kernel,
        grid=(n // tn,),
        in_specs=[pl.BlockSpec((8, D_MODEL), lambda j: (0, 0)),
                  pl.BlockSpec((D_MODEL, tn), lambda j: (0, j)),
                  pl.BlockSpec((1, tn), lambda j: (0, j))],
        out_specs=pl.BlockSpec((8, tn), lambda j: (0, j)),
        out_shape=jax.ShapeDtypeStruct((8, n), F32),
        compiler_params=_cparams(("parallel",)),
        name="mod",
    )(cc, w_mod, b_mod)


def _rope(acc, cos, sin_signed, first_half):
    out = []
    for c in range(acc.shape[1] // LANES):
        a = acc[:, c * LANES:(c + 1) * LANES]
        rot = jnp.where(first_half, pltpu.roll(a, LANES - ROPE_FREQS, 1), pltpu.roll(a, ROPE_FREQS, 1))
        out.append(a * cos + rot * sin_signed)
    return jnp.concatenate(out, axis=1)


def _inproj_kernel(x_ref, g_ref, sh_ref, sc_ref, w_ref, wgt_ref, cos_ref, sin_ref,
                   p_ref, vt_ref, gt_ref, xn_ref, *, rope, tk_v):
    j = pl.program_id(1)
    tm = x_ref.shape[0]

    @pl.when(j == 0)
    def _():
        x = x_ref[...]
        y = _rms(x, g_ref[...]) * (1.0 + sc_ref[0]) + sh_ref[0]
        xn = y.astype(BF16)
        xn_ref[...] = xn
        gt = _dot_nt(wgt_ref[...], xn)
        for c in range(tm // CHUNK):
            gt_ref[c] = gt[:, c * CHUNK:(c + 1) * CHUNK]

    acc = _dot(xn_ref[...], w_ref[...])

    is_rope = (j == TILE_DQ) | (j == TILE_DK)
    is_v = j == TILE_DV

    if rope:
        @pl.when(is_rope)
        def _():
            lane = lax.broadcasted_iota(jnp.int32, (tm, LANES), 1)
            first_half = (lane % (2 * ROPE_FREQS)) < ROPE_FREQS
            fac = jnp.where(j == TILE_DQ, QK_SCALE_LOG2E, 1.0)
            p_ref[...] = (_rope(acc, cos_ref[...], sin_ref[...], first_half) * fac).astype(BF16)

        plain = jnp.logical_not(is_rope | is_v)
    else:
        plain = jnp.logical_not(is_v)

    @pl.when(plain)
    def _():
        p_ref[...] = acc.astype(BF16)

    @pl.when(is_v)
    def _():
        at = acc.T.astype(BF16)
        for c in range(tm // tk_v):
            vt_ref[c] = at[:, c * tk_v:(c + 1) * tk_v]


def _inproj_call(x2d, g, sh, sc, wp, wgt, cos_t, sin_t, *, tm, rows_per_batch, rope, tk_v):
    m = x2d.shape[0]
    blocks_per_batch = rows_per_batch // tm
    kern = functools.partial(_inproj_kernel, rope=rope, tk_v=tk_v)
    return pl.pallas_call(
        kern,
        grid=(m // tm, N_PROJ_TILES),
        in_specs=[
            pl.BlockSpec((tm, D_MODEL), lambda i, j: (i, 0)),
            pl.BlockSpec((1, D_MODEL), lambda i, j: (0, 0)),
            pl.BlockSpec((1, 1, D_MODEL), lambda i, j: (i // blocks_per_batch, 0, 0)),
            pl.BlockSpec((1, 1, D_MODEL), lambda i, j: (i // blocks_per_batch, 0, 0)),
            pl.BlockSpec((D_MODEL, GROUP_WIDTH), lambda i, j: (0, j)),
            pl.BlockSpec((N_GATES, D_MODEL), lambda i, j: (0, 0)),
            pl.BlockSpec((tm, LANES), lambda i, j: (i % blocks_per_batch, 0)),
            pl.BlockSpec((tm, LANES), lambda i, j: (i % blocks_per_batch, 0)),
        ],
        out_specs=[
            pl.BlockSpec((tm, GROUP_WIDTH), lambda i, j: (i, jnp.minimum(j, P_TILES - 1))),
            pl.BlockSpec((tm // tk_v, GROUP_WIDTH, tk_v), lambda i, j: (i, 0, 0)),
            pl.BlockSpec((tm // CHUNK, N_GATES, CHUNK), lambda i, j: (i, 0, 0)),
        ],
        out_shape=[
            jax.ShapeDtypeStruct((m, P_TILES * GROUP_WIDTH), BF16),
            jax.ShapeDtypeStruct((m // tk_v, GROUP_WIDTH, tk_v), BF16),
            jax.ShapeDtypeStruct((m // CHUNK, N_GATES, CHUNK), F32),
        ],
        scratch_shapes=[pltpu.VMEM((tm, D_MODEL), BF16)],
        compiler_params=_cparams(("parallel", "arbitrary")),
        name="inproj_rope" if rope else "inproj_ctx",
    )(x2d, g, sh, sc, wp, wgt, cos_t, sin_t)


def _log_sigmoid(x):
    return jnp.minimum(x, 0.0) - jnp.log1p(jnp.exp(-jnp.abs(x)))


def _split_dot(row, tri):
    hi = row.astype(BF16)
    lo = (row - hi.astype(F32)).astype(BF16)
    hi8 = jnp.broadcast_to(hi, (BF16_SUBLANES, CHUNK))
    lo8 = jnp.broadcast_to(lo, (BF16_SUBLANES, CHUNK))
    return (_dot(hi8, tri) + _dot(lo8, tri))[0:1, :]


def _mlstm_kernel(bg_ref,
                  q_ref, k_ref, v_ref, o_ref, gt_ref,
                  kc_ref, vc_ref, gtc_ref,
                  cwq_ref, cwk_ref, cbq_ref, cbk_ref, gn_ref,
                  out_ref,
                  qs_ref, kst_ref, gs_ref, kstc_ref, gsc_ref, hf_ref, hb_ref, cst_ref):
    h = pl.program_id(1)
    n_lat = q_ref.shape[0] // CHUNK
    n_ctx = kc_ref.shape[0] // CHUNK

    row_i = lax.broadcasted_iota(jnp.int32, (CHUNK, CHUNK), 0)
    col_i = lax.broadcasted_iota(jnp.int32, (CHUNK, CHUNK), 1)
    tril = col_i <= row_i
    triu = col_i >= row_i
    tril_b = tril.astype(BF16)
    triu_b = triu.astype(BF16)
    ones_col = (col_i == 0).astype(BF16)
    first_row = row_i == 0
    last_row = row_i == CHUNK - 1

    def conv_silu(ref, c, n_chunks, w_ref, b_ref):
        r0 = pl.multiple_of(c * CHUNK, CHUNK)
        main = ref[pl.ds(r0, CHUNK), :].astype(F32)
        pstart = pl.multiple_of(jnp.maximum(r0 - BF16_SUBLANES, 0), BF16_SUBLANES)
        nstart = pl.multiple_of(jnp.minimum(r0 + CHUNK, (n_chunks - 1) * CHUNK), BF16_SUBLANES)
        prev_blk = ref[pl.ds(pstart, BF16_SUBLANES), :].astype(F32)
        next_blk = ref[pl.ds(nstart, BF16_SUBLANES), :].astype(F32)
        prev_row = prev_blk[BF16_SUBLANES - 1:BF16_SUBLANES, :] * jnp.where(c > 0, 1.0, 0.0)
        next_row = next_blk[0:1, :] * jnp.where(c < n_chunks - 1, 1.0, 0.0)
        up = jnp.where(first_row, prev_row, pltpu.roll(main, 1, 0))
        dn = jnp.where(last_row, next_row, pltpu.roll(main, CHUNK - 1, 0))
        w = w_ref[...]
        u = up * w[0:1, :] + main * w[1:2, :] + dn * w[2:3, :] + b_ref[...]
        return u * jax.nn.sigmoid(u)

    def gate_rows(g_ref3, dst_ref, c):
        def row(t):
            return g_ref3[c, pl.ds(t * HEADS + h, 1), :] + bg_ref[t, h]
        i_f, f_f, i_b, f_b = row(0), row(1), row(2), row(3)
        lf_f = _log_sigmoid(f_f)
        lf_b = _log_sigmoid(f_b)
        dst_ref[c, 0:1, :] = i_f - _split_dot(lf_f, triu_b)
        dst_ref[c, 1:2, :] = lf_f
        dst_ref[c, 2:3, :] = i_b - _split_dot(lf_b, tril_b)
        dst_ref[c, 3:4, :] = lf_b

    k_scale = HEAD_DIM ** -0.5

    def prep_lat(c, carry):
        qs_ref[c] = conv_silu(q_ref, c, n_lat, cwq_ref, cbq_ref).astype(BF16)
        kk = conv_silu(k_ref, c, n_lat, cwk_ref, cbk_ref) * k_scale
        kst_ref[c] = kk.T.astype(BF16)
        gate_rows(gt_ref, gs_ref, c)
        return carry

    lax.fori_loop(0, n_lat, prep_lat, 0, unroll=2)

    def prep_ctx(c, carry):
        kk = conv_silu(kc_ref, c, n_ctx, cwk_ref, cbk_ref) * k_scale
        kstc_ref[c] = kk.T.astype(BF16)
        gate_rows(gtc_ref, gsc_ref, c)
        return carry

    lax.fori_loop(0, n_ctx, prep_ctx, 0)

    def vaug(vref, c):
        r0 = pl.multiple_of(c * CHUNK, CHUNK)
        return jnp.concatenate([vref[pl.ds(r0, CHUNK), :], ones_col], axis=1)

    def state_update(d, m_prev, a_row, lf_row, kt, va):
        m_last = jnp.maximum(m_prev, jnp.max(a_row, axis=1, keepdims=True))
        b_last = jnp.sum(lf_row, axis=1, keepdims=True)
        w_row = jnp.exp(a_row - m_last)
        decay = jnp.exp(m_prev - m_last)
        kw = (kt.astype(F32) * w_row).astype(BF16)
        cst_ref[d] = decay * cst_ref[d] + _dot(kw, va)
        return b_last + m_last

    def chunk_step(d, c, m_prev, mask):
        a_row = gs_ref[c, 2 * d:2 * d + 1, :]
        lf_row = gs_ref[c, 2 * d + 1:2 * d + 2, :]
        q = qs_ref[c]
        kt = kst_ref[c]
        va = vaug(v_ref, c)
        a_mat = jnp.where(mask, a_row, -jnp.inf)
        m_col = jnp.maximum(jnp.max(a_mat, axis=1, keepdims=True), m_prev)
        w_mat = jnp.exp(a_mat - m_col)
        w_inter = jnp.exp(m_prev - m_col)
        b_col = jnp.sum(jnp.where(mask, lf_row, 0.0), axis=1, keepdims=True)
        s = (_dot(q, kt) * w_mat).astype(BF16)
        nd = w_inter * _dot(q, cst_ref[d].astype(BF16)) + _dot(s, va)
        num = nd[:, :HEAD_DIM]
        den = nd[:, HEAD_DIM:HEAD_DIM + 1]
        hh = num / jnp.maximum(jnp.abs(den), jnp.exp(-b_col - m_col))
        m_new = state_update(d, m_prev, a_row, lf_row, kt, va)
        return hh, m_new

    cst_ref[...] = jnp.zeros(cst_ref.shape, F32)

    def ctx_fwd(c, m):
        return state_update(0, m, gsc_ref[c, 0:1, :], gsc_ref[c, 1:2, :], kstc_ref[c], vaug(vc_ref, c))

    def ctx_bwd(i, m):
        c = n_ctx - 1 - i
        return state_update(1, m, gsc_ref[c, 2:3, :], gsc_ref[c, 3:4, :], kstc_ref[c], vaug(vc_ref, c))

    m_f = lax.fori_loop(0, n_ctx, ctx_fwd, jnp.zeros((1, 1), F32))
    m_b = lax.fori_loop(0, n_ctx, ctx_bwd, jnp.zeros((1, 1), F32))

    def main_step(i, carry):
        m_f, m_b = carry
        hf, m_f = chunk_step(0, i, m_f, tril)
        hf_ref[i] = hf
        cb = n_lat - 1 - i
        hb, m_b = chunk_step(1, cb, m_b, triu)
        hb_ref[cb] = hb
        return m_f, m_b

    lax.fori_loop(0, n_lat, main_step, (m_f, m_b), unroll=4)

    def finish(c, carry):
        r0 = pl.multiple_of(c * CHUNK, CHUNK)
        hh = _rms(hf_ref[c] + hb_ref[c], gn_ref[...])
        og = o_ref[pl.ds(r0, CHUNK), :].astype(F32)
        out_ref[pl.ds(r0, CHUNK), :] = (hh * jax.nn.sigmoid(og)).astype(BF16)
        return carry

    lax.fori_loop(0, n_lat, finish, 0, unroll=4)


def _mlstm_call(bg, p_lat, gt_lat, p_ctx, gt_ctx, cw, cb, gn, *, batch, seq):
    n_lat = seq // CHUNK
    n_ctx = CTX_LEN // CHUNK

    def col(tile):
        return lambda b, h: (b, tile * HEADS + h)

    lat_blk = (seq, HEAD_DIM)
    ctx_blk = (CTX_LEN, HEAD_DIM)
    return pl.pallas_call(
        _mlstm_kernel,
        grid=(batch, HEADS),
        in_specs=[
            pl.BlockSpec(memory_space=pltpu.SMEM),
            pl.BlockSpec(lat_blk, col(TILE_MQ)),
            pl.BlockSpec(lat_blk, col(TILE_MK)),
            pl.BlockSpec(lat_blk, col(TILE_MV)),
            pl.BlockSpec(lat_blk, col(TILE_MO)),
            pl.BlockSpec((n_lat, N_GATES, CHUNK), lambda b, h: (b, 0, 0)),
            pl.BlockSpec(ctx_blk, col(TILE_MK)),
            pl.BlockSpec(ctx_blk, col(TILE_MV)),
            pl.BlockSpec((n_ctx, N_GATES, CHUNK), lambda b, h: (b, 0, 0)),
            pl.BlockSpec((3, HEAD_DIM), lambda b, h: (0, h)),
            pl.BlockSpec((3, HEAD_DIM), lambda b, h: (0, HEADS + h)),
            pl.BlockSpec((1, HEAD_DIM), lambda b, h: (0, h)),
            pl.BlockSpec((1, HEAD_DIM), lambda b, h: (0, HEADS + h)),
            pl.BlockSpec((1, HEAD_DIM), lambda b, h: (0, h)),
        ],
        out_specs=pl.BlockSpec(lat_blk, lambda b, h: (b, h)),
        out_shape=jax.ShapeDtypeStruct((batch * seq, GROUP_WIDTH), BF16),
        scratch_shapes=[
            pltpu.VMEM((n_lat, CHUNK, HEAD_DIM), BF16),
            pltpu.VMEM((n_lat, HEAD_DIM, CHUNK), BF16),
            pltpu.VMEM((n_lat, 8, CHUNK), F32),
            pltpu.VMEM((n_ctx, HEAD_DIM, CHUNK), BF16),
            pltpu.VMEM((n_ctx, 8, CHUNK), F32),
            pltpu.VMEM((n_lat, CHUNK, HEAD_DIM), F32),
            pltpu.VMEM((n_lat, CHUNK, HEAD_DIM), F32),
            pltpu.VMEM((2, HEAD_DIM, 2 * HEAD_DIM), F32),
        ],
        compiler_params=_cparams(("parallel", "parallel")),
        name="mlstm",
    )(bg, p_lat, p_lat, p_lat, p_lat, gt_lat, p_ctx, p_ctx, gt_ctx, cw, cw, cb, cb, gn)


def _attn_kernel(q_ref, k_ref, vt_ref, kc_ref, vtc_ref, lam_ref, gd_ref, o_ref,
                 qq_ref, m_ref, acc_ref, s0_ref, s1_ref, cm0_ref, cm1_ref,
                 p0_ref, p1_ref, al0_ref, al1_ref, *, lam_init):
    tq = q_ref.shape[0]
    n_kv = k_ref.shape[0] // ATT_TK

    lane = lax.broadcasted_iota(jnp.int32, (tq, HEAD_DIM), 1)
    q = q_ref[...]
    zero = jnp.zeros_like(q)
    qq_ref[0:tq, :] = jnp.where(lane < DIFF_QK_DIM, q, zero)
    qq_ref[tq:2 * tq, :] = jnp.where(lane >= DIFF_QK_DIM, q, zero)
    m_ref[...] = jnp.full(m_ref.shape, -jnp.inf, F32)
    ones_rows = jnp.ones((BF16_SUBLANES, ATT_TK), BF16)
    acc_ref[...] = jnp.zeros(acc_ref.shape, F32)

    sbufs = ((s0_ref, cm0_ref), (s1_ref, cm1_ref))
    pbufs = ((p0_ref, al0_ref), (p1_ref, al1_ref))

    strips = [slice(c * ATT_STRIP, (c + 1) * ATT_STRIP) for c in range(2 * tq // ATT_STRIP)]

    def score(k, buf, cs):
        s_ref, cm_ref = sbufs[buf]
        st = _dot_nt(k, qq_ref[cs, :])
        s_ref[:, cs] = st
        cm_ref[:, cs] = jnp.max(st, axis=0, keepdims=True)

    def softmax(buf, cs):
        s_ref, cm_ref = sbufs[buf]
        p_ref, al_ref = pbufs[buf]
        m_prev = m_ref[:, cs]
        m_new = jnp.maximum(m_prev, cm_ref[:, cs])
        p_ref[:, cs] = jnp.exp2((s_ref[:, cs] - m_new).astype(BF16))
        al_ref[:, cs] = jnp.exp2(m_prev - m_new)
        m_ref[:, cs] = m_new

    def pv(vt_aug, buf, cs):
        p_ref, al_ref = pbufs[buf]
        acc_ref[:, cs] = al_ref[:, cs] * acc_ref[:, cs] + _dot(vt_aug, p_ref[:, cs])

    def substep(k, score_buf, soft_buf, vt, pv_buf):
        vt_aug = None if vt is None else jnp.concatenate([vt, ones_rows], axis=0)
        for cs in strips:
            if k is not None:
                score(k, score_buf, cs)
            if soft_buf is not None:
                softmax(soft_buf, cs)
            if vt is not None:
                pv(vt_aug, pv_buf, cs)

    def k_lat(t):
        return k_ref[pl.ds(pl.multiple_of(t * ATT_TK, ATT_TK), ATT_TK), :]

    substep(kc_ref[...], 0, None, None, None)
    substep(k_lat(0), 1, 0, None, None)
    substep(k_lat(1), 0, 1, vtc_ref[0], 0)

    def body(i, carry):
        t = 2 * i
        substep(k_lat(t + 2), 1, 0, vt_ref[t], 1)
        substep(k_lat(t + 3), 0, 1, vt_ref[t + 1], 0)
        return carry

    lax.fori_loop(0, n_kv // 2 - 1, body, 0, unroll=3)
    substep(None, None, 0, vt_ref[n_kv - 2], 1)
    substep(None, None, None, vt_ref[n_kv - 1], 0)

    lp = lam_ref[...]
    e1 = jnp.exp(jnp.sum(lp[0:1, :] * lp[1:2, :], axis=1, keepdims=True))
    e2 = jnp.exp(jnp.sum(lp[2:3, :] * lp[3:4, :], axis=1, keepdims=True))
    lam = e1 - e2 + lam_init
    o = acc_ref[0:HEAD_DIM, :] / acc_ref[HEAD_DIM:HEAD_DIM + 1, :]
    od = o[:, 0:tq] - lam * o[:, tq:2 * tq]
    od = od * lax.rsqrt(jnp.mean(od * od, axis=0, keepdims=True) + EPS)
    o_ref[...] = (od.T * gd_ref[...] * (1.0 - lam_init)).astype(BF16)


def _attn_call(p_lat, vt_lat, p_ctx, vt_ctx, lamp, gd, *, batch, seq, lam_init):
    nq = seq // ATT_TQ
    n_kv = seq // ATT_TK
    kern = functools.partial(_attn_kernel, lam_init=lam_init)
    return pl.pallas_call(
        kern,
        grid=(batch, HEADS, nq),
        in_specs=[
            pl.BlockSpec((ATT_TQ, HEAD_DIM), lambda b, h, i: (b * nq + i, TILE_DQ * HEADS + h)),
            pl.BlockSpec((seq, HEAD_DIM), lambda b, h, i: (b, TILE_DK * HEADS + h)),
            pl.BlockSpec((n_kv, HEAD_DIM, ATT_TK), lambda b, h, i: (b, h, 0)),
            pl.BlockSpec((CTX_LEN, HEAD_DIM), lambda b, h, i: (b, TILE_DK * HEADS + h)),
            pl.BlockSpec((1, HEAD_DIM, CTX_LEN), lambda b, h, i: (b, h, 0)),
            pl.BlockSpec((8, LANES), lambda b, h, i: (0, 0)),
            pl.BlockSpec((1, HEAD_DIM), lambda b, h, i: (0, 0)),
        ],
        out_specs=pl.BlockSpec((ATT_TQ, HEAD_DIM), lambda b, h, i: (b * nq + i, h)),
        out_shape=jax.ShapeDtypeStruct((batch * seq, GROUP_WIDTH), BF16),
        scratch_shapes=[
            pltpu.VMEM((2 * ATT_TQ, HEAD_DIM), BF16),
            pltpu.VMEM((1, 2 * ATT_TQ), F32),
            pltpu.VMEM((HEAD_DIM + BF16_SUBLANES, 2 * ATT_TQ), F32),
            pltpu.VMEM((ATT_TK, 2 * ATT_TQ), F32),
            pltpu.VMEM((ATT_TK, 2 * ATT_TQ), F32),
            pltpu.VMEM((1, 2 * ATT_TQ), F32),
            pltpu.VMEM((1, 2 * ATT_TQ), F32),
            pltpu.VMEM((ATT_TK, 2 * ATT_TQ), BF16),
            pltpu.VMEM((ATT_TK, 2 * ATT_TQ), BF16),
            pltpu.VMEM((1, 2 * ATT_TQ), F32),
            pltpu.VMEM((1, 2 * ATT_TQ), F32),
        ],
        compiler_params=_cparams(("parallel", "parallel", "arbitrary")),
        name="diffattn",
    )(p_lat, p_lat, vt_lat, p_ctx, vt_ctx, lamp, gd)


def _outproj_kernel(m_ref, d_ref, wa_ref, wb_ref, x_ref, ga_ref, sh_ref, sc_ref, gpost_ref, gpre_ref,
                    x1_ref, h_ref):
    y = _dot(m_ref[...], wa_ref[...]) + _dot(d_ref[...], wb_ref[...])
    x1 = x_ref[...] + ga_ref[0] * _rms(y, gpost_ref[...])
    x1_ref[...] = x1
    hh = _rms(x1, gpre_ref[...]) * (1.0 + sc_ref[0]) + sh_ref[0]
    h_ref[...] = hh.astype(BF16)


def _outproj_call(m_lat, d_lat, wa, wb, x2d, ga1, sh2, sc2, gpost, gpre, *, tm, seq):
    m = x2d.shape[0]
    bpb = seq // tm
    row = lambda i: (i, 0)
    fixed = lambda i: (0, 0)
    per_batch = lambda i: (i // bpb, 0, 0)
    return pl.pallas_call(
        _outproj_kernel,
        grid=(m // tm,),
        in_specs=[
            pl.BlockSpec((tm, GROUP_WIDTH), row),
            pl.BlockSpec((tm, GROUP_WIDTH), row),
            pl.BlockSpec((GROUP_WIDTH, D_MODEL), fixed),
            pl.BlockSpec((GROUP_WIDTH, D_MODEL), fixed),
            pl.BlockSpec((tm, D_MODEL), row),
            pl.BlockSpec((1, 1, D_MODEL), per_batch),
            pl.BlockSpec((1, 1, D_MODEL), per_batch),
            pl.BlockSpec((1, 1, D_MODEL), per_batch),
            pl.BlockSpec((1, D_MODEL), fixed),
            pl.BlockSpec((1, D_MODEL), fixed),
        ],
        out_specs=[pl.BlockSpec((tm, D_MODEL), row), pl.BlockSpec((tm, D_MODEL), row)],
        out_shape=[jax.ShapeDtypeStruct((m, D_MODEL), F32), jax.ShapeDtypeStruct((m, D_MODEL), BF16)],
        compiler_params=_cparams(("parallel",)),
        name="outproj",
    )(m_lat, d_lat, wa, wb, x2d, ga1, sh2, sc2, gpost, gpre)


def _ffn_kernel(h_ref, hp_ref, hn_ref, wg_ref, wv_ref, cwg_ref, cwv_ref, cbg_ref, cbv_ref, wd_ref,
                x1_ref, ga_ref, gpost_ref, o_ref, lhs_ref, acc_ref, ug_ref, uv_ref, a_ref, *, blocks_per_seq):
    i = pl.program_id(0)
    f = pl.program_id(1)
    tm = h_ref.shape[0]
    halo = BF16_SUBLANES

    @pl.when(f == 0)
    def _():
        lhs_ref[0:halo, :] = hp_ref[...]
        lhs_ref[halo:halo + tm, :] = h_ref[...]
        lhs_ref[halo + tm:2 * halo + tm, :] = hn_ref[...]
        acc_ref[...] = jnp.zeros(acc_ref.shape, F32)

    has_prev = ((i % blocks_per_seq) > 0).astype(F32)
    has_next = ((i % blocks_per_seq) < blocks_per_seq - 1).astype(F32)

    def up_project(w_ref, u_ref):
        u_ref[...] = _dot(lhs_ref[...], w_ref[...])
        u_ref[halo - 1:halo, :] = u_ref[halo - 1:halo, :] * has_prev
        u_ref[halo + tm:halo + tm + 1, :] = u_ref[halo + tm:halo + tm + 1, :] * has_next

    up_project(wg_ref, ug_ref)
    up_project(wv_ref, uv_ref)

    def conv(u_ref, r, cw, cb):
        lo = halo + r
        return (u_ref[lo - 1:lo - 1 + FFN_ROWS, :] * cw[0:1, :] + u_ref[lo:lo + FFN_ROWS, :] * cw[1:2, :]
                + u_ref[lo + 1:lo + 1 + FFN_ROWS, :] * cw[2:3, :] + cb)

    cwg, cwv, cbg, cbv = cwg_ref[...], cwv_ref[...], cbg_ref[...], cbv_ref[...]
    for r in range(0, tm, FFN_ROWS):
        gate = conv(ug_ref, r, cwg, cbg)
        val = conv(uv_ref, r, cwv, cbv)
        a_ref[r:r + FFN_ROWS, :] = (gate * jax.nn.sigmoid(gate) * val).astype(BF16)
    acc_ref[...] += _dot(a_ref[...], wd_ref[...])

    @pl.when(f == pl.num_programs(1) - 1)
    def _():
        o_ref[...] = x1_ref[...] + ga_ref[0] * _rms(acc_ref[...], gpost_ref[...])


def _ffn_call(hff, w_up, cw, cb, w_down, x1, ga2, gpost, *, tm, tf, seq):
    m = hff.shape[0]
    nf = D_FF // tf
    bps = seq // tm
    halo = BF16_SUBLANES
    hb = tm // halo
    n_halo = m // halo
    kern = functools.partial(_ffn_kernel, blocks_per_seq=bps)
    row = lambda i, f: (i, 0)
    fixed = lambda i, f: (0, 0)
    return pl.pallas_call(
        kern,
        grid=(m // tm, nf),
        in_specs=[
            pl.BlockSpec((tm, D_MODEL), row),
            pl.BlockSpec((halo, D_MODEL), lambda i, f: (jnp.maximum(i * hb - 1, 0), 0)),
            pl.BlockSpec((halo, D_MODEL), lambda i, f: (jnp.minimum((i + 1) * hb, n_halo - 1), 0)),
            pl.BlockSpec((D_MODEL, tf), lambda i, f: (0, f)),
            pl.BlockSpec((D_MODEL, tf), lambda i, f: (0, nf + f)),
            pl.BlockSpec((3, tf), lambda i, f: (0, f)),
            pl.BlockSpec((3, tf), lambda i, f: (0, nf + f)),
            pl.BlockSpec((1, tf), lambda i, f: (0, f)),
            pl.BlockSpec((1, tf), lambda i, f: (0, nf + f)),
            pl.BlockSpec((tf, D_MODEL), lambda i, f: (f, 0)),
            pl.BlockSpec((tm, D_MODEL), row),
            pl.BlockSpec((1, 1, D_MODEL), lambda i, f: (i // bps, 0, 0)),
            pl.BlockSpec((1, D_MODEL), fixed),
        ],
        out_specs=pl.BlockSpec((tm, D_MODEL), row),
        out_shape=jax.ShapeDtypeStruct((m, D_MODEL), F32),
        scratch_shapes=[pltpu.VMEM((tm + 2 * halo, D_MODEL), BF16), pltpu.VMEM((tm, D_MODEL), F32),
                        pltpu.VMEM((tm + 2 * halo, tf), F32), pltpu.VMEM((tm + 2 * halo, tf), F32),
                        pltpu.VMEM((tm, tf), BF16)],
        compiler_params=_cparams(("parallel", "arbitrary")),
        name="convffn",
    )(hff, hff, hff, w_up, w_up, cw, cw, cb, cb, w_down, x1, ga2, gpost)


def _rope_tables(seq):
    rows = seq // GRID_W
    r = jnp.repeat(jnp.arange(rows, dtype=F32), GRID_W)
    c = jnp.tile(jnp.arange(GRID_W, dtype=F32), rows)
    inv_freq = ROPE_BASE ** (-jnp.arange(ROPE_FREQS, dtype=F32) / ROPE_FREQS)
    ar = r[:, None] * inv_freq
    ac = c[:, None] * inv_freq
    ang = jnp.concatenate([ar, ar, ac, ac, ar, ar, ac, ac], axis=1)
    sign = jnp.tile(jnp.concatenate([-jnp.ones(ROPE_FREQS, F32), jnp.ones(ROPE_FREQS, F32)]), 4)
    return jnp.cos(ang), jnp.sin(ang) * sign


def kernel(x, c, ctx, c_ctx, w_mod, b_mod, g_pre_mix, g_post_mix, w_in, b_gate, conv_qk_w, conv_qk_b,
           g_mlstm, lambda_q1, lambda_k1, lambda_q2, lambda_k2, g_diff, w_out, g_pre_ffn, g_post_ffn,
           w_up, conv_ffn_w, conv_ffn_b, w_down):
    batch, seq, d = x.shape
    depth = w_mod.shape[0]
    assert depth == 1 and d == D_MODEL and ctx.shape[1] == CTX_LEN
    l = 0
    lam_init = 0.8 - 0.6 * math.exp(-0.3 * l)

    cc = jnp.zeros((8, d), F32).at[:batch].set(c).at[batch].set(c_ctx)
    mod = _mod_call(cc, w_mod[l], b_mod[l][None, :])
    mod = mod.reshape(8, N_MOD, 1, d)
    sh1, sc1, ga1, sh2, sc2, ga2 = (mod[:batch, k] for k in range(N_MOD))
    csh1 = jnp.broadcast_to(mod[batch, 0][None], (batch, 1, d))
    csc1 = jnp.broadcast_to(mod[batch, 1][None], (batch, 1, d))

    wl = w_in[l]
    wp = jnp.concatenate([wl[:, :OFF_MG], wl[:, OFF_DQ:]], axis=1).astype(BF16)
    wgt = wl[:, OFF_MG:OFF_DQ].T.astype(BF16)

    cos_t, sin_t = _rope_tables(seq)
    g_pre = g_pre_mix[l][None, :]
    x2d = x.reshape(batch * seq, d)
    ctx2d = ctx.reshape(batch * CTX_LEN, d)

    p_lat, vt_lat, gt_lat = _inproj_call(x2d, g_pre, sh1, sc1, wp, wgt, cos_t, sin_t,
                                         tm=512, rows_per_batch=seq, rope=True, tk_v=ATT_TK)
    p_ctx, vt_ctx, gt_ctx = _inproj_call(ctx2d, g_pre, csh1, csc1, wp, wgt, cos_t, sin_t,
                                         tm=CTX_LEN, rows_per_batch=CTX_LEN, rope=False, tk_v=CTX_LEN)

    m_lat = _mlstm_call(b_gate[l], p_lat, gt_lat, p_ctx, gt_ctx, conv_qk_w[l], conv_qk_b[l][None, :],
                        g_mlstm[l][None, :], batch=batch, seq=seq)

    lamp = jnp.zeros((8, LANES), F32)
    lamp = lamp.at[0, :DIFF_QK_DIM].set(lambda_q1[l]).at[1, :DIFF_QK_DIM].set(lambda_k1[l])
    lamp = lamp.at[2, :DIFF_QK_DIM].set(lambda_q2[l]).at[3, :DIFF_QK_DIM].set(lambda_k2[l])
    d_lat = _attn_call(p_lat, vt_lat, p_ctx, vt_ctx, lamp, g_diff[l][None, :],
                       batch=batch, seq=seq, lam_init=lam_init)

    wo = w_out[l].astype(BF16)
    x1, hff = _outproj_call(m_lat, d_lat, wo[:GROUP_WIDTH], wo[GROUP_WIDTH:], x2d, ga1, sh2, sc2,
                            g_post_mix[l][None, :], g_pre_ffn[l][None, :], tm=512, seq=seq)

    x2 = _ffn_call(hff, w_up[l].astype(BF16), conv_ffn_w[l], conv_ffn_b[l][None, :],
                   w_down[l].astype(BF16), x1, ga2, g_post_ffn[l][None, :], tm=512, tf=512, seq=seq)
    return x2.reshape(batch, seq, d)
```

```python
import functools
import math

import jax
import jax.numpy as jnp
import numpy as np
from jax import lax
from jax.experimental import pallas as pl
from jax.experimental.pallas import tpu as pltpu

F32 = jnp.float32
BF16 = jnp.bfloat16

D_MODEL = 2048
CTX_LEN = 256
GRID_W = 64
HEADS = 8
HEAD_DIM = 128
GROUP_WIDTH = HEADS * HEAD_DIM
CHUNK = 128
DIFF_QK_DIM = 64
ROPE_FREQS = 16
ROPE_BASE = 10000.0
D_FF = 5632
N_MOD = 6
EPS = 1e-6
N_GATES = 4 * HEADS

OFF_MG = 4 * GROUP_WIDTH
OFF_DQ = OFF_MG + N_GATES

LANES = 128
BF16_SUBLANES = 16
MXU_COLS = 256
VMEM_LIMIT = 56 * 1024 * 1024

TILE_MQ, TILE_MK, TILE_MV, TILE_MO, TILE_DQ, TILE_DK, TILE_DV = range(7)
N_PROJ_TILES = 7
P_TILES = 6

ATT_TQ = 1024
ATT_TK = 512
ATT_STRIP = 256
FFN_ROWS = 64
QK_SCALE_LOG2E = DIFF_QK_DIM ** -0.5 * 1.4426950408889634


def _cparams(sem, flags=None):
    return pltpu.CompilerParams(dimension_semantics=sem, vmem_limit_bytes=VMEM_LIMIT, flags=flags)


def _dot(a, b):
    return jnp.dot(a, b, preferred_element_type=F32)


def _dot_nt(a, b):
    return lax.dot_general(a, b, (((1,), (1,)), ((), ())), preferred_element_type=F32)


def _rms(y, g):
    return y * lax.rsqrt(jnp.mean(y * y, axis=-1, keepdims=True) + EPS) * g


def _mod_kernel(c_ref, w_ref, b_ref, o_ref):
    cc = c_ref[...]
    s = cc * jax.nn.sigmoid(cc)
    o_ref[...] = _dot(s.astype(BF16), w_ref[...].astype(BF16)) + b_ref[...]


def _mod_call(cc, w_mod, b_mod):
    n = w_mod.shape[1]
    tn = 1024
    return pl.pallas_call(
        _mod_kernel,
        grid=(n // tn,),
        in_specs=[pl.BlockSpec((8, D_MODEL), lambda j: (0, 0)),
                  pl.BlockSpec((D_MODEL, tn), lambda j: (0, j)),
                  pl.BlockSpec((1, tn), lambda j: (0, j))],
        out_specs=pl.BlockSpec((8, tn), lambda j: (0, j)),
        out_shape=jax.ShapeDtypeStruct((8, n), F32),
        compiler_params=_cparams(("parallel",)),
        name="mod",
    )(cc, w_mod, b_mod)


def _rope(acc, cos, sin_signed, first_half):
    out = []
    for c in range(acc.shape[1] // LANES):
        a = acc[:, c * LANES:(c + 1) * LANES]
        rot = jnp.where(first_half, pltpu.roll(a, LANES - ROPE_FREQS, 1), pltpu.roll(a, ROPE_FREQS, 1))
        out.append(a * cos + rot * sin_signed)
    return jnp.concatenate(out, axis=1)


def _inproj_kernel(x_ref, g_ref, sh_ref, sc_ref, w_ref, wgt_ref, cos_ref, sin_ref,
                   p_ref, vt_ref, gt_ref, xn_ref, *, rope, tk_v):
    j = pl.program_id(1)
    tm = x_ref.shape[0]

    @pl.when(j == 0)
    def _():
        x = x_ref[...]
        y = _rms(x, g_ref[...]) * (1.0 + sc_ref[0]) + sh_ref[0]
        xn = y.astype(BF16)
        xn_ref[...] = xn
        gt = _dot(xn, wgt_ref[...]).T[0:N_GATES, :]
        for c in range(tm // CHUNK):
            gt_ref[c] = gt[:, c * CHUNK:(c + 1) * CHUNK]

    acc = _dot(xn_ref[...], w_ref[...])

    is_rope = (j == TILE_DQ) | (j == TILE_DK)
    is_v = j == TILE_DV

    if rope:
        @pl.when(is_rope)
        def _():
            lane = lax.broadcasted_iota(jnp.int32, (tm, LANES), 1)
            first_half = (lane % (2 * ROPE_FREQS)) < ROPE_FREQS
            fac = jnp.where(j == TILE_DQ, QK_SCALE_LOG2E, 1.0)
            p_ref[...] = (_rope(acc, cos_ref[...], sin_ref[...], first_half) * fac).astype(BF16)

        plain = jnp.logical_not(is_rope | is_v)
    else:
        plain = jnp.logical_not(is_v)

    @pl.when(plain)
    def _():
        p_ref[...] = acc.astype(BF16)

    @pl.when(is_v)
    def _():
        at = acc.T.astype(BF16)
        for c in range(tm // tk_v):
            vt_ref[c] = at[:, c * tk_v:(c + 1) * tk_v]


def _inproj_call(x2d, g, sh, sc, wp, wgt, cos_t, sin_t, *, tm, rows_per_batch, rope, tk_v):
    m = x2d.shape[0]
    blocks_per_batch = rows_per_batch // tm
    kern = functools.partial(_inproj_kernel, rope=rope, tk_v=tk_v)
    return pl.pallas_call(
        kern,
        grid=(m // tm, N_PROJ_TILES),
        in_specs=[
            pl.BlockSpec((tm, D_MODEL), lambda i, j: (i, 0)),
            pl.BlockSpec((1, D_MODEL), lambda i, j: (0, 0)),
            pl.BlockSpec((1, 1, D_MODEL), lambda i, j: (i // blocks_per_batch, 0, 0)),
            pl.BlockSpec((1, 1, D_MODEL), lambda i, j: (i // blocks_per_batch, 0, 0)),
            pl.BlockSpec((D_MODEL, GROUP_WIDTH), lambda i, j: (0, j)),
            pl.BlockSpec((D_MODEL, LANES), lambda i, j: (0, 0)),
            pl.BlockSpec((tm, LANES), lambda i, j: (i % blocks_per_batch, 0)),
            pl.BlockSpec((tm, LANES), lambda i, j: (i % blocks_per_batch, 0)),
        ],
        out_specs=[
            pl.BlockSpec((tm, GROUP_WIDTH), lambda i, j: (i, jnp.minimum(j, P_TILES - 1))),
            pl.BlockSpec((tm // tk_v, GROUP_WIDTH, tk_v), lambda i, j: (i, 0, 0)),
            pl.BlockSpec((tm // CHUNK, N_GATES, CHUNK), lambda i, j: (i, 0, 0)),
        ],
        out_shape=[
            jax.ShapeDtypeStruct((m, P_TILES * GROUP_WIDTH), BF16),
            jax.ShapeDtypeStruct((m // tk_v, GROUP_WIDTH, tk_v), BF16),
            jax.ShapeDtypeStruct((m // CHUNK, N_GATES, CHUNK), F32),
        ],
        scratch_shapes=[pltpu.VMEM((tm, D_MODEL), BF16)],
        compiler_params=_cparams(("parallel", "arbitrary")),
        name="inproj_rope" if rope else "inproj_ctx",
    )(x2d, g, sh, sc, wp, wgt, cos_t, sin_t)


def _log_sigmoid(x):
    return jnp.minimum(x, 0.0) - jnp.log1p(jnp.exp(-jnp.abs(x)))


def _split_dot(row, tri):
    hi = row.astype(BF16)
    lo = (row - hi.astype(F32)).astype(BF16)
    hi8 = jnp.broadcast_to(hi, (BF16_SUBLANES, CHUNK))
    lo8 = jnp.broadcast_to(lo, (BF16_SUBLANES, CHUNK))
    return (_dot(hi8, tri) + _dot(lo8, tri))[0:1, :]


def _mlstm_kernel(bg_ref,
                  q_ref, k_ref, v_ref, o_ref, gt_ref,
                  kc_ref, vc_ref, gtc_ref,
                  cwq_ref, cwk_ref, cbq_ref, cbk_ref, gn_ref,
                  out_ref,
                  qs_ref, kst_ref, gs_ref, kstc_ref, gsc_ref, hf_ref, hb_ref, cst_ref):
    h = pl.program_id(1)
    n_lat = q_ref.shape[0] // CHUNK
    n_ctx = kc_ref.shape[0] // CHUNK

    row_i = lax.broadcasted_iota(jnp.int32, (CHUNK, CHUNK), 0)
    col_i = lax.broadcasted_iota(jnp.int32, (CHUNK, CHUNK), 1)
    tril = col_i <= row_i
    triu = col_i >= row_i
    tril_b = tril.astype(BF16)
    triu_b = triu.astype(BF16)
    ones_col = (col_i == 0).astype(BF16)
    first_row = row_i == 0
    last_row = row_i == CHUNK - 1

    def conv_silu(ref, c, n_chunks, w_ref, b_ref):
        r0 = pl.multiple_of(c * CHUNK, CHUNK)
        main = ref[pl.ds(r0, CHUNK), :].astype(F32)
        pstart = pl.multiple_of(jnp.maximum(r0 - BF16_SUBLANES, 0), BF16_SUBLANES)
        nstart = pl.multiple_of(jnp.minimum(r0 + CHUNK, (n_chunks - 1) * CHUNK), BF16_SUBLANES)
        prev_blk = ref[pl.ds(pstart, BF16_SUBLANES), :].astype(F32)
        next_blk = ref[pl.ds(nstart, BF16_SUBLANES), :].astype(F32)
        prev_row = prev_blk[BF16_SUBLANES - 1:BF16_SUBLANES, :] * jnp.where(c > 0, 1.0, 0.0)
        next_row = next_blk[0:1, :] * jnp.where(c < n_chunks - 1, 1.0, 0.0)
        up = jnp.where(first_row, prev_row, pltpu.roll(main, 1, 0))
        dn = jnp.where(last_row, next_row, pltpu.roll(main, CHUNK - 1, 0))
        w = w_ref[...]
        u = up * w[0:1, :] + main * w[1:2, :] + dn * w[2:3, :] + b_ref[...]
        return u * jax.nn.sigmoid(u)

    def gate_rows(g_ref3, dst_ref, c):
        def row(t):
            return g_ref3[c, pl.ds(t * HEADS + h, 1), :] + bg_ref[t, h]
        i_f, f_f, i_b, f_b = row(0), row(1), row(2), row(3)
        lf_f = _log_sigmoid(f_f)
        lf_b = _log_sigmoid(f_b)
        dst_ref[c, 0:1, :] = i_f - _split_dot(lf_f, triu_b)
        dst_ref[c, 1:2, :] = lf_f
        dst_ref[c, 2:3, :] = i_b - _split_dot(lf_b, tril_b)
        dst_ref[c, 3:4, :] = lf_b

    k_scale = HEAD_DIM ** -0.5

    def prep_lat(c, carry):
        qs_ref[c] = conv_silu(q_ref, c, n_lat, cwq_ref, cbq_ref).astype(BF16)
        kk = conv_silu(k_ref, c, n_lat, cwk_ref, cbk_ref) * k_scale
        kst_ref[c] = kk.T.astype(BF16)
        gate_rows(gt_ref, gs_ref, c)
        return carry

    lax.fori_loop(0, n_lat, prep_lat, 0, unroll=2)

    def prep_ctx(c, carry):
        kk = conv_silu(kc_ref, c, n_ctx, cwk_ref, cbk_ref) * k_scale
        kstc_ref[c] = kk.T.astype(BF16)
        gate_rows(gtc_ref, gsc_ref, c)
        return carry

    lax.fori_loop(0, n_ctx, prep_ctx, 0)

    def vaug(vref, c):
        r0 = pl.multiple_of(c * CHUNK, CHUNK)
        return jnp.concatenate([vref[pl.ds(r0, CHUNK), :], ones_col], axis=1)

    def state_update(d, m_prev, a_row, lf_row, kt, va):
        m_last = jnp.maximum(m_prev, jnp.max(a_row, axis=1, keepdims=True))
        b_last = jnp.sum(lf_row, axis=1, keepdims=True)
        w_row = jnp.exp(a_row - m_last)
        decay = jnp.exp(m_prev - m_last)
        kw = (kt.astype(F32) * w_row).astype(BF16)
        cst_ref[d] = decay * cst_ref[d] + _dot(kw, va)
        return b_last + m_last

    def chunk_step(d, c, m_prev, mask):
        a_row = gs_ref[c, 2 * d:2 * d + 1, :]
        lf_row = gs_ref[c, 2 * d + 1:2 * d + 2, :]
        q = qs_ref[c]
        kt = kst_ref[c]
        va = vaug(v_ref, c)
        a_mat = jnp.where(mask, a_row, -jnp.inf)
        m_col = jnp.maximum(jnp.max(a_mat, axis=1, keepdims=True), m_prev)
        w_mat = jnp.exp(a_mat - m_col)
        w_inter = jnp.exp(m_prev - m_col)
        b_col = jnp.sum(jnp.where(mask, lf_row, 0.0), axis=1, keepdims=True)
        s = (_dot(q, kt) * w_mat).astype(BF16)
        nd = w_inter * _dot(q, cst_ref[d].astype(BF16)) + _dot(s, va)
        num = nd[:, :HEAD_DIM]
        den = nd[:, HEAD_DIM:HEAD_DIM + 1]
        hh = num / jnp.maximum(jnp.abs(den), jnp.exp(-b_col - m_col))
        m_new = state_update(d, m_prev, a_row, lf_row, kt, va)
        return hh, m_new

    cst_ref[...] = jnp.zeros(cst_ref.shape, F32)

    def ctx_fwd(c, m):
        return state_update(0, m, gsc_ref[c, 0:1, :], gsc_ref[c, 1:2, :], kstc_ref[c], vaug(vc_ref, c))

    def ctx_bwd(i, m):
        c = n_ctx - 1 - i
        return state_update(1, m, gsc_ref[c, 2:3, :], gsc_ref[c, 3:4, :], kstc_ref[c], vaug(vc_ref, c))

    m_f = lax.fori_loop(0, n_ctx, ctx_fwd, jnp.zeros((1, 1), F32))
    m_b = lax.fori_loop(0, n_ctx, ctx_bwd, jnp.zeros((1, 1), F32))

    def main_step(i, carry):
        m_f, m_b = carry
        hf, m_f = chunk_step(0, i, m_f, tril)
        hf_ref[i] = hf
        cb = n_lat - 1 - i
        hb, m_b = chunk_step(1, cb, m_b, triu)
        hb_ref[cb] = hb
        return m_f, m_b

    lax.fori_loop(0, n_lat, main_step, (m_f, m_b), unroll=8)

    def finish(c, carry):
        r0 = pl.multiple_of(c * CHUNK, CHUNK)
        hh = _rms(hf_ref[c] + hb_ref[c], gn_ref[...])
        og = o_ref[pl.ds(r0, CHUNK), :].astype(F32)
        out_ref[pl.ds(r0, CHUNK), :] = (hh * jax.nn.sigmoid(og)).astype(BF16)
        return carry

    lax.fori_loop(0, n_lat, finish, 0, unroll=4)


def _mlstm_call(bg, p_lat, gt_lat, p_ctx, gt_ctx, cw, cb, gn, *, batch, seq):
    n_lat = seq // CHUNK
    n_ctx = CTX_LEN // CHUNK

    def col(tile):
        return lambda b, h: (b, tile * HEADS + h)

    lat_blk = (seq, HEAD_DIM)
    ctx_blk = (CTX_LEN, HEAD_DIM)
    return pl.pallas_call(
        _mlstm_kernel,
        grid=(batch, HEADS),
        in_specs=[
            pl.BlockSpec(memory_space=pltpu.SMEM),
            pl.BlockSpec(lat_blk, col(TILE_MQ)),
            pl.BlockSpec(lat_blk, col(TILE_MK)),
            pl.BlockSpec(lat_blk, col(TILE_MV)),
            pl.BlockSpec(lat_blk, col(TILE_MO)),
            pl.BlockSpec((n_lat, N_GATES, CHUNK), lambda b, h: (b, 0, 0)),
            pl.BlockSpec(ctx_blk, col(TILE_MK)),
            pl.BlockSpec(ctx_blk, col(TILE_MV)),
            pl.BlockSpec((n_ctx, N_GATES, CHUNK), lambda b, h: (b, 0, 0)),
            pl.BlockSpec((3, HEAD_DIM), lambda b, h: (0, h)),
            pl.BlockSpec((3, HEAD_DIM), lambda b, h: (0, HEADS + h)),
            pl.BlockSpec((1, HEAD_DIM), lambda b, h: (0, h)),
            pl.BlockSpec((1, HEAD_DIM), lambda b, h: (0, HEADS + h)),
            pl.BlockSpec((1, HEAD_DIM), lambda b, h: (0, h)),
        ],
        out_specs=pl.BlockSpec(lat_blk, lambda b, h: (b, h)),
        out_shape=jax.ShapeDtypeStruct((batch * seq, GROUP_WIDTH), BF16),
        scratch_shapes=[
            pltpu.VMEM((n_lat, CHUNK, HEAD_DIM), BF16),
            pltpu.VMEM((n_lat, HEAD_DIM, CHUNK), BF16),
            pltpu.VMEM((n_lat, 8, CHUNK), F32),
            pltpu.VMEM((n_ctx, HEAD_DIM, CHUNK), BF16),
            pltpu.VMEM((n_ctx, 8, CHUNK), F32),
            pltpu.VMEM((n_lat, CHUNK, HEAD_DIM), F32),
            pltpu.VMEM((n_lat, CHUNK, HEAD_DIM), F32),
            pltpu.VMEM((2, HEAD_DIM, 2 * HEAD_DIM), F32),
        ],
        compiler_params=_cparams(("parallel", "parallel")),
        name="mlstm",
    )(bg, p_lat, p_lat, p_lat, p_lat, gt_lat, p_ctx, p_ctx, gt_ctx, cw, cw, cb, cb, gn)


def _attn_kernel(q_ref, k_ref, vt_ref, kc_ref, vtc_ref, lam_ref, gd_ref, o_ref,
                 qq_ref, m_ref, acc_ref, s0_ref, s1_ref, cm0_ref, cm1_ref,
                 p0_ref, p1_ref, al0_ref, al1_ref, *, lam_init):
    tq = q_ref.shape[0]
    n_kv = k_ref.shape[0] // ATT_TK

    lane = lax.broadcasted_iota(jnp.int32, (tq, HEAD_DIM), 1)
    q = q_ref[...]
    zero = jnp.zeros_like(q)
    qq_ref[0:tq, :] = jnp.where(lane < DIFF_QK_DIM, q, zero)
    qq_ref[tq:2 * tq, :] = jnp.where(lane >= DIFF_QK_DIM, q, zero)
    m_ref[...] = jnp.full(m_ref.shape, -jnp.inf, F32)
    acc_ref[...] = jnp.zeros(acc_ref.shape, F32)

    sbufs = ((s0_ref, cm0_ref), (s1_ref, cm1_ref))
    pbufs = ((p0_ref, al0_ref), (p1_ref, al1_ref))

    strips = [slice(c * ATT_STRIP, (c + 1) * ATT_STRIP) for c in range(2 * tq // ATT_STRIP)]

    def score(k, buf, cs):
        s_ref, cm_ref = sbufs[buf]
        st = _dot_nt(k, qq_ref[cs, :])
        s_ref[0:k.shape[0], cs] = st
        cm_ref[:, cs] = jnp.max(st, axis=0, keepdims=True)

    def softmax(buf, rows, cs):
        s_ref, cm_ref = sbufs[buf]
        p_ref, al_ref = pbufs[buf]
        m_prev = m_ref[:, cs]
        m_new = jnp.maximum(m_prev, cm_ref[:, cs])
        p_ref[0:rows, cs] = jnp.exp2((s_ref[0:rows, cs] - m_new).astype(BF16))
        al_ref[:, cs] = jnp.exp2(m_prev - m_new)
        m_ref[:, cs] = m_new

    def pv(vt_aug, buf, cs):
        p_ref, al_ref = pbufs[buf]
        rows = vt_aug.shape[1]
        acc_ref[:, cs] = al_ref[:, cs] * acc_ref[:, cs] + _dot(vt_aug, p_ref[0:rows, cs])

    def substep(k, score_buf, soft_buf, soft_rows, vt, pv_buf):
        if vt is not None:
            vt_aug = jnp.concatenate([vt, jnp.ones((BF16_SUBLANES, vt.shape[1]), BF16)], axis=0)
        for cs in strips:
            if k is not None:
                score(k, score_buf, cs)
            if soft_buf is not None:
                softmax(soft_buf, soft_rows, cs)
            if vt is not None:
                pv(vt_aug, pv_buf, cs)

    def k_lat(t):
        return k_ref[pl.ds(pl.multiple_of(t * ATT_TK, ATT_TK), ATT_TK), :]

    substep(kc_ref[...], 0, None, None, None, None)
    substep(k_lat(0), 1, 0, CTX_LEN, None, None)
    substep(k_lat(1), 0, 1, ATT_TK, vtc_ref[0], 0)

    def body(i, carry):
        t = 2 * i
        substep(k_lat(t + 2), 1, 0, ATT_TK, vt_ref[t], 1)
        substep(k_lat(t + 3), 0, 1, ATT_TK, vt_ref[t + 1], 0)
        return carry

    lax.fori_loop(0, n_kv // 2 - 1, body, 0)
    substep(None, None, 0, ATT_TK, vt_ref[n_kv - 2], 1)
    substep(None, None, None, None, vt_ref[n_kv - 1], 0)

    lp = lam_ref[...]
    e1 = jnp.exp(jnp.sum(lp[0:1, :] * lp[1:2, :], axis=1, keepdims=True))
    e2 = jnp.exp(jnp.sum(lp[2:3, :] * lp[3:4, :], axis=1, keepdims=True))
    lam = e1 - e2 + lam_init
    o = acc_ref[0:HEAD_DIM, :] / acc_ref[HEAD_DIM:HEAD_DIM + 1, :]
    od = o[:, 0:tq] - lam * o[:, tq:2 * tq]
    od = od * lax.rsqrt(jnp.mean(od * od, axis=0, keepdims=True) + EPS)
    o_ref[...] = (od.T * gd_ref[...] * (1.0 - lam_init)).astype(BF16)


def _attn_call(p_lat, vt_lat, p_ctx, vt_ctx, lamp, gd, *, batch, seq, lam_init):
    nq = seq // ATT_TQ
    n_kv = seq // ATT_TK
    kern = functools.partial(_attn_kernel, lam_init=lam_init)
    return pl.pallas_call(
        kern,
        grid=(batch, HEADS, nq),
        in_specs=[
            pl.BlockSpec((ATT_TQ, HEAD_DIM), lambda b, h, i: (b * nq + i, TILE_DQ * HEADS + h)),
            pl.BlockSpec((seq, HEAD_DIM), lambda b, h, i: (b, TILE_DK * HEADS + h)),
            pl.BlockSpec((n_kv, HEAD_DIM, ATT_TK), lambda b, h, i: (b, h, 0)),
            pl.BlockSpec((CTX_LEN, HEAD_DIM), lambda b, h, i: (b, TILE_DK * HEADS + h)),
            pl.BlockSpec((1, HEAD_DIM, CTX_LEN), lambda b, h, i: (b, h, 0)),
            pl.BlockSpec((8, LANES), lambda b, h, i: (0, 0)),
            pl.BlockSpec((1, HEAD_DIM), lambda b, h, i: (0, 0)),
        ],
        out_specs=pl.BlockSpec((ATT_TQ, HEAD_DIM), lambda b, h, i: (b * nq + i, h)),
        out_shape=jax.ShapeDtypeStruct((batch * seq, GROUP_WIDTH), BF16),
        scratch_shapes=[
            pltpu.VMEM((2 * ATT_TQ, HEAD_DIM), BF16),
            pltpu.VMEM((1, 2 * ATT_TQ), F32),
            pltpu.VMEM((HEAD_DIM + BF16_SUBLANES, 2 * ATT_TQ), F32),
            pltpu.VMEM((ATT_TK, 2 * ATT_TQ), F32),
            pltpu.VMEM((ATT_TK, 2 * ATT_TQ), F32),
            pltpu.VMEM((1, 2 * ATT_TQ), F32),
            pltpu.VMEM((1, 2 * ATT_TQ), F32),
            pltpu.VMEM((ATT_TK, 2 * ATT_TQ), BF16),
            pltpu.VMEM((ATT_TK, 2 * ATT_TQ), BF16),
            pltpu.VMEM((1, 2 * ATT_TQ), F32),
            pltpu.VMEM((1, 2 * ATT_TQ), F32),
        ],
        compiler_params=_cparams(("parallel", "parallel", "arbitrary")),
        name="diffattn",
    )(p_lat, p_lat, vt_lat, p_ctx, vt_ctx, lamp, gd)


def _outproj_kernel(m_ref, d_ref, wa_ref, wb_ref, x_ref, ga_ref, sh_ref, sc_ref, gpost_ref, gpre_ref,
                    x1_ref, h_ref):
    y = _dot(m_ref[...], wa_ref[...]) + _dot(d_ref[...], wb_ref[...])
    x1 = x_ref[...] + ga_ref[0] * _rms(y, gpost_ref[...])
    x1_ref[...] = x1
    hh = _rms(x1, gpre_ref[...]) * (1.0 + sc_ref[0]) + sh_ref[0]
    h_ref[...] = hh.astype(BF16)


def _outproj_call(m_lat, d_lat, wa, wb, x2d, ga1, sh2, sc2, gpost, gpre, *, tm, seq):
    m = x2d.shape[0]
    bpb = seq // tm
    row = lambda i: (i, 0)
    fixed = lambda i: (0, 0)
    per_batch = lambda i: (i // bpb, 0, 0)
    return pl.pallas_call(
        _outproj_kernel,
        grid=(m // tm,),
        in_specs=[
            pl.BlockSpec((tm, GROUP_WIDTH), row),
            pl.BlockSpec((tm, GROUP_WIDTH), row),
            pl.BlockSpec((GROUP_WIDTH, D_MODEL), fixed),
            pl.BlockSpec((GROUP_WIDTH, D_MODEL), fixed),
            pl.BlockSpec((tm, D_MODEL), row),
            pl.BlockSpec((1, 1, D_MODEL), per_batch),
            pl.BlockSpec((1, 1, D_MODEL), per_batch),
            pl.BlockSpec((1, 1, D_MODEL), per_batch),
            pl.BlockSpec((1, D_MODEL), fixed),
            pl.BlockSpec((1, D_MODEL), fixed),
        ],
        out_specs=[pl.BlockSpec((tm, D_MODEL), row), pl.BlockSpec((tm, D_MODEL), row)],
        out_shape=[jax.ShapeDtypeStruct((m, D_MODEL), F32), jax.ShapeDtypeStruct((m, D_MODEL), BF16)],
        compiler_params=_cparams(("parallel",)),
        name="outproj",
    )(m_lat, d_lat, wa, wb, x2d, ga1, sh2, sc2, gpost, gpre)


def _ffn_kernel(h_ref, hp_ref, hn_ref, wg_ref, wv_ref, cwg_ref, cwv_ref, cbg_ref, cbv_ref, wd_ref,
                x1_ref, ga_ref, gpost_ref, o_ref, lhs_ref, acc_ref, ug_ref, uv_ref, a_ref, *, blocks_per_seq):
    i = pl.program_id(0)
    f = pl.program_id(1)
    tm = h_ref.shape[0]
    halo = BF16_SUBLANES

    @pl.when(f == 0)
    def _():
        lhs_ref[0:halo, :] = hp_ref[...]
        lhs_ref[halo:halo + tm, :] = h_ref[...]
        lhs_ref[halo + tm:2 * halo + tm, :] = hn_ref[...]
        acc_ref[...] = jnp.zeros(acc_ref.shape, F32)

    has_prev = ((i % blocks_per_seq) > 0).astype(F32)
    has_next = ((i % blocks_per_seq) < blocks_per_seq - 1).astype(F32)

    def up_project(w_ref, u_ref):
        u_ref[...] = _dot(lhs_ref[...], w_ref[...])
        u_ref[halo - 1:halo, :] = u_ref[halo - 1:halo, :] * has_prev
        u_ref[halo + tm:halo + tm + 1, :] = u_ref[halo + tm:halo + tm + 1, :] * has_next

    up_project(wg_ref, ug_ref)
    up_project(wv_ref, uv_ref)

    def conv(u_ref, r, cw, cb):
        lo = halo + r
        return (u_ref[lo - 1:lo - 1 + FFN_ROWS, :] * cw[0:1, :] + u_ref[lo:lo + FFN_ROWS, :] * cw[1:2, :]
                + u_ref[lo + 1:lo + 1 + FFN_ROWS, :] * cw[2:3, :] + cb)

    cwg, cwv, cbg, cbv = cwg_ref[...], cwv_ref[...], cbg_ref[...], cbv_ref[...]
    half = tm // 2
    for r0 in (0, half):
        for r in range(r0, r0 + half, FFN_ROWS):
            gate = conv(ug_ref, r, cwg, cbg)
            val = conv(uv_ref, r, cwv, cbv)
            a_ref[r:r + FFN_ROWS, :] = (gate * jax.nn.sigmoid(gate) * val).astype(BF16)
        acc_ref[r0:r0 + half, :] += _dot(a_ref[r0:r0 + half, :], wd_ref[...])

    @pl.when(f == pl.num_programs(1) - 1)
    def _():
        o_ref[...] = x1_ref[...] + ga_ref[0] * _rms(acc_ref[...], gpost_ref[...])


def _ffn_call(hff, w_up, cw, cb, w_down, x1, ga2, gpost, *, tm, tf, seq):
    m = hff.shape[0]
    nf = D_FF // tf
    bps = seq // tm
    halo = BF16_SUBLANES
    hb = tm // halo
    n_halo = m // halo
    kern = functools.partial(_ffn_kernel, blocks_per_seq=bps)
    row = lambda i, f: (i, 0)
    fixed = lambda i, f: (0, 0)
    return pl.pallas_call(
        kern,
        grid=(m // tm, nf),
        in_specs=[
            pl.BlockSpec((tm, D_MODEL), row),
            pl.BlockSpec((halo, D_MODEL), lambda i, f: (jnp.maximum(i * hb - 1, 0), 0)),
            pl.BlockSpec((halo, D_MODEL), lambda i, f: (jnp.minimum((i + 1) * hb, n_halo - 1), 0)),
            pl.BlockSpec((D_MODEL, tf), lambda i, f: (0, f)),
            pl.BlockSpec((D_MODEL, tf), lambda i, f: (0, nf + f)),
            pl.BlockSpec((3, tf), lambda i, f: (0, f)),
            pl.BlockSpec((3, tf), lambda i, f: (0, nf + f)),
            pl.BlockSpec((1, tf), lambda i, f: (0, f)),
            pl.BlockSpec((1, tf), lambda i, f: (0, nf + f)),
            pl.BlockSpec((tf, D_MODEL), lambda i, f: (f, 0)),
            pl.BlockSpec((tm, D_MODEL), row),
            pl.BlockSpec((1, 1, D_MODEL), lambda i, f: (i // bps, 0, 0)),
            pl.BlockSpec((1, D_MODEL), fixed),
        ],
        out_specs=pl.BlockSpec((tm, D_MODEL), row),
        out_shape=jax.ShapeDtypeStruct((m, D_MODEL), F32),
        scratch_shapes=[pltpu.VMEM((tm + 2 * halo, D_MODEL), BF16), pltpu.VMEM((tm, D_MODEL), F32),
                        pltpu.VMEM((tm + 2 * halo, tf), F32), pltpu.VMEM((tm + 2 * halo, tf), F32),
                        pltpu.VMEM((tm, tf), BF16)],
        compiler_params=_cparams(("parallel", "arbitrary")),
        name="convffn",
    )(hff, hff, hff, w_up, w_up, cw, cw, cb, cb, w_down, x1, ga2, gpost)


def _rope_tables(seq):
    rows = seq // GRID_W
    r = np.repeat(np.arange(rows, dtype=np.float64), GRID_W)
    c = np.tile(np.arange(GRID_W, dtype=np.float64), rows)
    inv_freq = ROPE_BASE ** (-np.arange(ROPE_FREQS, dtype=np.float64) / ROPE_FREQS)
    ar = r[:, None] * inv_freq
    ac = c[:, None] * inv_freq
    ang = np.concatenate([ar, ar, ac, ac, ar, ar, ac, ac], axis=1)
    sign = np.tile(np.concatenate([-np.ones(ROPE_FREQS), np.ones(ROPE_FREQS)]), 4)
    return jnp.asarray(np.cos(ang), F32), jnp.asarray(np.sin(ang) * sign, F32)


def kernel(x, c, ctx, c_ctx, w_mod, b_mod, g_pre_mix, g_post_mix, w_in, b_gate, conv_qk_w, conv_qk_b,
           g_mlstm, lambda_q1, lambda_k1, lambda_q2, lambda_k2, g_diff, w_out, g_pre_ffn, g_post_ffn,
           w_up, conv_ffn_w, conv_ffn_b, w_down):
    batch, seq, d = x.shape
    depth = w_mod.shape[0]
    assert depth == 1 and d == D_MODEL and ctx.shape[1] == CTX_LEN
    l = 0
    lam_init = 0.8 - 0.6 * math.exp(-0.3 * l)

    cc = jnp.zeros((8, d), F32).at[:batch].set(c).at[batch].set(c_ctx)
    mod = _mod_call(cc, w_mod[l], b_mod[l][None, :])
    mod = mod.reshape(8, N_MOD, 1, d)
    sh1, sc1, ga1, sh2, sc2, ga2 = (mod[:batch, k] for k in range(N_MOD))
    csh1 = jnp.broadcast_to(mod[batch, 0][None], (batch, 1, d))
    csc1 = jnp.broadcast_to(mod[batch, 1][None], (batch, 1, d))

    wl = w_in[l]
    wp = jnp.concatenate([wl[:, :OFF_MG], wl[:, OFF_DQ:]], axis=1).astype(BF16)
    wgt = jnp.pad(wl[:, OFF_MG:OFF_DQ], ((0, 0), (0, LANES - N_GATES))).astype(BF16)

    cos_t, sin_t = _rope_tables(seq)
    g_pre = g_pre_mix[l][None, :]
    x2d = x.reshape(batch * seq, d)
    ctx2d = ctx.reshape(batch * CTX_LEN, d)

    p_lat, vt_lat, gt_lat = _inproj_call(x2d, g_pre, sh1, sc1, wp, wgt, cos_t, sin_t,
                                         tm=512, rows_per_batch=seq, rope=True, tk_v=ATT_TK)
    p_ctx, vt_ctx, gt_ctx = _inproj_call(ctx2d, g_pre, csh1, csc1, wp, wgt, cos_t, sin_t,
                                         tm=CTX_LEN, rows_per_batch=CTX_LEN, rope=False, tk_v=CTX_LEN)

    m_lat = _mlstm_call(b_gate[l], p_lat, gt_lat, p_ctx, gt_ctx, conv_qk_w[l], conv_qk_b[l][None, :],
                        g_mlstm[l][None, :], batch=batch, seq=seq)

    lamp = jnp.zeros((8, LANES), F32)
    lamp = lamp.at[0, :DIFF_QK_DIM].set(lambda_q1[l]).at[1, :DIFF_QK_DIM].set(lambda_k1[l])
    lamp = lamp.at[2, :DIFF_QK_DIM].set(lambda_q2[l]).at[3, :DIFF_QK_DIM].set(lambda_k2[l])
    d_lat = _attn_call(p_lat, vt_lat, p_ctx, vt_ctx, lamp, g_diff[l][None, :],
                       batch=batch, seq=seq, lam_init=lam_init)

    wo = w_out[l].astype(BF16)
    x1, hff = _outproj_call(m_lat, d_lat, wo[:GROUP_WIDTH], wo[GROUP_WIDTH:], x2d, ga1, sh2, sc2,
                            g_post_mix[l][None, :], g_pre_ffn[l][None, :], tm=512, seq=seq)

    x2 = _ffn_call(hff, w_up[l].astype(BF16), conv_ffn_w[l], conv_ffn_b[l][None, :],
                   w_down[l].astype(BF16), x1, ga2, g_post_ffn[l][None, :], tm=512, tf=512, seq=seq)
    return x2.reshape(batch, seq, d)
```

```python
import functools
import math

import jax
import jax.numpy as jnp
import numpy as np
from jax import lax
from jax.experimental import pallas as pl
from jax.experimental.pallas import tpu as pltpu

F32 = jnp.float32
BF16 = jnp.bfloat16

D_MODEL = 2048
CTX_LEN = 256
GRID_W = 64
HEADS = 8
HEAD_DIM = 128
GROUP_WIDTH = HEADS * HEAD_DIM
CHUNK = 128
DIFF_QK_DIM = 64
ROPE_FREQS = 16
ROPE_BASE = 10000.0
D_FF = 5632
N_MOD = 6
EPS = 1e-6
N_GATES = 4 * HEADS

OFF_MG = 4 * GROUP_WIDTH
OFF_DQ = OFF_MG + N_GATES

LANES = 128
BF16_SUBLANES = 16
MXU_COLS = 256
VMEM_LIMIT = 56 * 1024 * 1024

TILE_MQ, TILE_MK, TILE_MV, TILE_MO, TILE_DQ, TILE_DK, TILE_DV = range(7)
N_PROJ_TILES = 7
P_TILES = 6

ATT_TQ = 1024
ATT_TK = 512
ATT_STRIP = 256
FFN_ROWS = 64
SHIFT_HEADROOM_LOG2 = 60.0
QK_SCALE_LOG2E = DIFF_QK_DIM ** -0.5 * 1.4426950408889634


def _cparams(sem, flags=None):
    return pltpu.CompilerParams(dimension_semantics=sem, vmem_limit_bytes=VMEM_LIMIT, flags=flags)


def _dot(a, b):
    return jnp.dot(a, b, preferred_element_type=F32)


def _dot_nt(a, b):
    return lax.dot_general(a, b, (((1,), (1,)), ((), ())), preferred_element_type=F32)


def _rms(y, g):
    return y * lax.rsqrt(jnp.mean(y * y, axis=-1, keepdims=True) + EPS) * g


def _mod_kernel(c_ref, w_ref, b_ref, o_ref):
    cc = c_ref[...]
    s = cc * jax.nn.sigmoid(cc)
    o_ref[...] = _dot(s.astype(BF16), w_ref[...].astype(BF16)) + b_ref[...]


def _mod_call(cc, w_mod, b_mod):
    n = w_mod.shape[1]
    tn = 1024
    return pl.pallas_call(
        _mod_kernel,
        grid=(n // tn,),
        in_specs=[pl.BlockSpec((8, D_MODEL), lambda j: (0, 0)),
                  pl.BlockSpec((D_MODEL, tn), lambda j: (0, j)),
                  pl.BlockSpec((1, tn), lambda j: (0, j))],
        out_specs=pl.BlockSpec((8, tn), lambda j: (0, j)),
        out_shape=jax.ShapeDtypeStruct((8, n), F32),
        compiler_params=_cparams(("parallel",)),
        name="mod",
    )(cc, w_mod, b_mod)


def _rope(acc, cos, sin_signed, first_half):
    out = []
    for c in range(acc.shape[1] // LANES):
        a = acc[:, c * LANES:(c + 1) * LANES]
        rot = jnp.where(first_half, pltpu.roll(a, LANES - ROPE_FREQS, 1), pltpu.roll(a, ROPE_FREQS, 1))
        out.append(a * cos + rot * sin_signed)
    return jnp.concatenate(out, axis=1)


def _inproj_kernel(x_ref, g_ref, sh_ref, sc_ref, w_ref, wgt_ref, cos_ref, sin_ref,
                   p_ref, vt_ref, gt_ref, xn_ref, *, rope, tk_v):
    j = pl.program_id(1)
    tm = x_ref.shape[0]

    @pl.when(j == 0)
    def _():
        x = x_ref[...]
        y = _rms(x, g_ref[...]) * (1.0 + sc_ref[0]) + sh_ref[0]
        xn = y.astype(BF16)
        xn_ref[...] = xn
        gt = _dot(xn, wgt_ref[...]).T[0:N_GATES, :]
        for c in range(tm // CHUNK):
            gt_ref[c] = gt[:, c * CHUNK:(c + 1) * CHUNK]

    acc = _dot(xn_ref[...], w_ref[...])

    is_rope = (j == TILE_DQ) | (j == TILE_DK)
    is_v = j == TILE_DV

    if rope:
        @pl.when(is_rope)
        def _():
            lane = lax.broadcasted_iota(jnp.int32, (tm, LANES), 1)
            first_half = (lane % (2 * ROPE_FREQS)) < ROPE_FREQS
            fac = jnp.where(j == TILE_DQ, QK_SCALE_LOG2E, 1.0)
            p_ref[...] = (_rope(acc, cos_ref[...], sin_ref[...], first_half) * fac).astype(BF16)

        plain = jnp.logical_not(is_rope | is_v)
    else:
        plain = jnp.logical_not(is_v)

    @pl.when(plain)
    def _():
        p_ref[...] = acc.astype(BF16)

    @pl.when(is_v)
    def _():
        at = acc.T.astype(BF16)
        for c in range(tm // tk_v):
            vt_ref[c] = at[:, c * tk_v:(c + 1) * tk_v]


def _inproj_call(x2d, g, sh, sc, wp, wgt, cos_t, sin_t, *, tm, rows_per_batch, rope, tk_v):
    m = x2d.shape[0]
    blocks_per_batch = rows_per_batch // tm
    kern = functools.partial(_inproj_kernel, rope=rope, tk_v=tk_v)
    return pl.pallas_call(
        kern,
        grid=(m // tm, N_PROJ_TILES),
        in_specs=[
            pl.BlockSpec((tm, D_MODEL), lambda i, j: (i, 0)),
            pl.BlockSpec((1, D_MODEL), lambda i, j: (0, 0)),
            pl.BlockSpec((1, 1, D_MODEL), lambda i, j: (i // blocks_per_batch, 0, 0)),
            pl.BlockSpec((1, 1, D_MODEL), lambda i, j: (i // blocks_per_batch, 0, 0)),
            pl.BlockSpec((D_MODEL, GROUP_WIDTH), lambda i, j: (0, j)),
            pl.BlockSpec((D_MODEL, LANES), lambda i, j: (0, 0)),
            pl.BlockSpec((tm, LANES), lambda i, j: (i % blocks_per_batch, 0)),
            pl.BlockSpec((tm, LANES), lambda i, j: (i % blocks_per_batch, 0)),
        ],
        out_specs=[
            pl.BlockSpec((tm, GROUP_WIDTH), lambda i, j: (i, jnp.minimum(j, P_TILES - 1))),
            pl.BlockSpec((tm // tk_v, GROUP_WIDTH, tk_v), lambda i, j: (i, 0, 0)),
            pl.BlockSpec((tm // CHUNK, N_GATES, CHUNK), lambda i, j: (i, 0, 0)),
        ],
        out_shape=[
            jax.ShapeDtypeStruct((m, P_TILES * GROUP_WIDTH), BF16),
            jax.ShapeDtypeStruct((m // tk_v, GROUP_WIDTH, tk_v), BF16),
            jax.ShapeDtypeStruct((m // CHUNK, N_GATES, CHUNK), F32),
        ],
        scratch_shapes=[pltpu.VMEM((tm, D_MODEL), BF16)],
        compiler_params=_cparams(("parallel", "arbitrary")),
        name="inproj_rope" if rope else "inproj_ctx",
    )(x2d, g, sh, sc, wp, wgt, cos_t, sin_t)


def _log_sigmoid(x):
    return jnp.minimum(x, 0.0) - jnp.log1p(jnp.exp(-jnp.abs(x)))


def _split_dot(row, tri):
    hi = row.astype(BF16)
    lo = (row - hi.astype(F32)).astype(BF16)
    hi8 = jnp.broadcast_to(hi, (BF16_SUBLANES, CHUNK))
    lo8 = jnp.broadcast_to(lo, (BF16_SUBLANES, CHUNK))
    return (_dot(hi8, tri) + _dot(lo8, tri))[0:1, :]


def _mlstm_kernel(bg_ref,
                  q_ref, k_ref, v_ref, o_ref, gt_ref,
                  kc_ref, vc_ref, gtc_ref,
                  cwq_ref, cwk_ref, cbq_ref, cbk_ref, gn_ref,
                  out_ref,
                  qs_ref, kst_ref, gs_ref, kstc_ref, gsc_ref, hf_ref, hb_ref, cst_ref):
    h = pl.program_id(1)
    n_lat = q_ref.shape[0] // CHUNK
    n_ctx = kc_ref.shape[0] // CHUNK

    row_i = lax.broadcasted_iota(jnp.int32, (CHUNK, CHUNK), 0)
    col_i = lax.broadcasted_iota(jnp.int32, (CHUNK, CHUNK), 1)
    tril = col_i <= row_i
    triu = col_i >= row_i
    tril_b = tril.astype(BF16)
    triu_b = triu.astype(BF16)
    ones_col = (col_i == 0).astype(BF16)
    first_row = row_i == 0
    last_row = row_i == CHUNK - 1

    def conv_silu(ref, c, n_chunks, w_ref, b_ref):
        r0 = pl.multiple_of(c * CHUNK, CHUNK)
        main = ref[pl.ds(r0, CHUNK), :].astype(F32)
        pstart = pl.multiple_of(jnp.maximum(r0 - BF16_SUBLANES, 0), BF16_SUBLANES)
        nstart = pl.multiple_of(jnp.minimum(r0 + CHUNK, (n_chunks - 1) * CHUNK), BF16_SUBLANES)
        prev_blk = ref[pl.ds(pstart, BF16_SUBLANES), :].astype(F32)
        next_blk = ref[pl.ds(nstart, BF16_SUBLANES), :].astype(F32)
        prev_row = prev_blk[BF16_SUBLANES - 1:BF16_SUBLANES, :] * jnp.where(c > 0, 1.0, 0.0)
        next_row = next_blk[0:1, :] * jnp.where(c < n_chunks - 1, 1.0, 0.0)
        up = jnp.where(first_row, prev_row, pltpu.roll(main, 1, 0))
        dn = jnp.where(last_row, next_row, pltpu.roll(main, CHUNK - 1, 0))
        w = w_ref[...]
        u = up * w[0:1, :] + main * w[1:2, :] + dn * w[2:3, :] + b_ref[...]
        return u * jax.nn.sigmoid(u)

    def gate_rows(g_ref3, dst_ref, c):
        def row(t):
            return g_ref3[c, pl.ds(t * HEADS + h, 1), :] + bg_ref[t, h]
        i_f, f_f, i_b, f_b = row(0), row(1), row(2), row(3)
        lf_f = _log_sigmoid(f_f)
        lf_b = _log_sigmoid(f_b)
        dst_ref[c, 0:1, :] = i_f - _split_dot(lf_f, triu_b)
        dst_ref[c, 1:2, :] = lf_f
        dst_ref[c, 2:3, :] = i_b - _split_dot(lf_b, tril_b)
        dst_ref[c, 3:4, :] = lf_b

    k_scale = HEAD_DIM ** -0.5

    def prep_lat(c, carry):
        qs_ref[c] = conv_silu(q_ref, c, n_lat, cwq_ref, cbq_ref).astype(BF16)
        kk = conv_silu(k_ref, c, n_lat, cwk_ref, cbk_ref) * k_scale
        kst_ref[c] = kk.T.astype(BF16)
        gate_rows(gt_ref, gs_ref, c)
        return carry

    lax.fori_loop(0, n_lat, prep_lat, 0, unroll=2)

    def prep_ctx(c, carry):
        kk = conv_silu(kc_ref, c, n_ctx, cwk_ref, cbk_ref) * k_scale
        kstc_ref[c] = kk.T.astype(BF16)
        gate_rows(gtc_ref, gsc_ref, c)
        return carry

    lax.fori_loop(0, n_ctx, prep_ctx, 0)

    def vaug(vref, c):
        r0 = pl.multiple_of(c * CHUNK, CHUNK)
        return jnp.concatenate([vref[pl.ds(r0, CHUNK), :], ones_col], axis=1)

    def state_update(d, m_prev, a_row, lf_row, kt, va):
        m_last = jnp.maximum(m_prev, jnp.max(a_row, axis=1, keepdims=True))
        b_last = jnp.sum(lf_row, axis=1, keepdims=True)
        w_row = jnp.exp(a_row - m_last)
        decay = jnp.exp(m_prev - m_last)
        kw = (kt.astype(F32) * w_row).astype(BF16)
        cst_ref[d] = decay * cst_ref[d] + _dot(kw, va)
        return b_last + m_last

    def chunk_step(d, c, m_prev, mask):
        a_row = gs_ref[c, 2 * d:2 * d + 1, :]
        lf_row = gs_ref[c, 2 * d + 1:2 * d + 2, :]
        q = qs_ref[c]
        kt = kst_ref[c]
        va = vaug(v_ref, c)
        a_mat = jnp.where(mask, a_row, -jnp.inf)
        m_col = jnp.maximum(jnp.max(a_mat, axis=1, keepdims=True), m_prev)
        w_mat = jnp.exp(a_mat - m_col)
        w_inter = jnp.exp(m_prev - m_col)
        b_col = jnp.sum(jnp.where(mask, lf_row, 0.0), axis=1, keepdims=True)
        s = (_dot(q, kt) * w_mat).astype(BF16)
        nd = w_inter * _dot(q, cst_ref[d].astype(BF16)) + _dot(s, va)
        num = nd[:, :HEAD_DIM]
        den = nd[:, HEAD_DIM:HEAD_DIM + 1]
        hh = num / jnp.maximum(jnp.abs(den), jnp.exp(-b_col - m_col))
        m_new = state_update(d, m_prev, a_row, lf_row, kt, va)
        return hh, m_new

    cst_ref[...] = jnp.zeros(cst_ref.shape, F32)

    def ctx_fwd(c, m):
        return state_update(0, m, gsc_ref[c, 0:1, :], gsc_ref[c, 1:2, :], kstc_ref[c], vaug(vc_ref, c))

    def ctx_bwd(i, m):
        c = n_ctx - 1 - i
        return state_update(1, m, gsc_ref[c, 2:3, :], gsc_ref[c, 3:4, :], kstc_ref[c], vaug(vc_ref, c))

    m_f = lax.fori_loop(0, n_ctx, ctx_fwd, jnp.zeros((1, 1), F32))
    m_b = lax.fori_loop(0, n_ctx, ctx_bwd, jnp.zeros((1, 1), F32))

    def main_step(i, carry):
        m_f, m_b = carry
        hf, m_f = chunk_step(0, i, m_f, tril)
        hf_ref[i] = hf
        cb = n_lat - 1 - i
        hb, m_b = chunk_step(1, cb, m_b, triu)
        hb_ref[cb] = hb
        return m_f, m_b

    lax.fori_loop(0, n_lat, main_step, (m_f, m_b), unroll=8)

    def finish(c, carry):
        r0 = pl.multiple_of(c * CHUNK, CHUNK)
        hh = _rms(hf_ref[c] + hb_ref[c], gn_ref[...])
        og = o_ref[pl.ds(r0, CHUNK), :].astype(F32)
        out_ref[pl.ds(r0, CHUNK), :] = (hh * jax.nn.sigmoid(og)).astype(BF16)
        return carry

    lax.fori_loop(0, n_lat, finish, 0, unroll=4)


def _mlstm_call(bg, p_lat, gt_lat, p_ctx, gt_ctx, cw, cb, gn, *, batch, seq):
    n_lat = seq // CHUNK
    n_ctx = CTX_LEN // CHUNK

    def col(tile):
        return lambda b, h: (b, tile * HEADS + h)

    lat_blk = (seq, HEAD_DIM)
    ctx_blk = (CTX_LEN, HEAD_DIM)
    return pl.pallas_call(
        _mlstm_kernel,
        grid=(batch, HEADS),
        in_specs=[
            pl.BlockSpec(memory_space=pltpu.SMEM),
            pl.BlockSpec(lat_blk, col(TILE_MQ)),
            pl.BlockSpec(lat_blk, col(TILE_MK)),
            pl.BlockSpec(lat_blk, col(TILE_MV)),
            pl.BlockSpec(lat_blk, col(TILE_MO)),
            pl.BlockSpec((n_lat, N_GATES, CHUNK), lambda b, h: (b, 0, 0)),
            pl.BlockSpec(ctx_blk, col(TILE_MK)),
            pl.BlockSpec(ctx_blk, col(TILE_MV)),
            pl.BlockSpec((n_ctx, N_GATES, CHUNK), lambda b, h: (b, 0, 0)),
            pl.BlockSpec((3, HEAD_DIM), lambda b, h: (0, h)),
            pl.BlockSpec((3, HEAD_DIM), lambda b, h: (0, HEADS + h)),
            pl.BlockSpec((1, HEAD_DIM), lambda b, h: (0, h)),
            pl.BlockSpec((1, HEAD_DIM), lambda b, h: (0, HEADS + h)),
            pl.BlockSpec((1, HEAD_DIM), lambda b, h: (0, h)),
        ],
        out_specs=pl.BlockSpec(lat_blk, lambda b, h: (b, h)),
        out_shape=jax.ShapeDtypeStruct((batch * seq, GROUP_WIDTH), BF16),
        scratch_shapes=[
            pltpu.VMEM((n_lat, CHUNK, HEAD_DIM), BF16),
            pltpu.VMEM((n_lat, HEAD_DIM, CHUNK), BF16),
            pltpu.VMEM((n_lat, 8, CHUNK), F32),
            pltpu.VMEM((n_ctx, HEAD_DIM, CHUNK), BF16),
            pltpu.VMEM((n_ctx, 8, CHUNK), F32),
            pltpu.VMEM((n_lat, CHUNK, HEAD_DIM), F32),
            pltpu.VMEM((n_lat, CHUNK, HEAD_DIM), F32),
            pltpu.VMEM((2, HEAD_DIM, 2 * HEAD_DIM), F32),
        ],
        compiler_params=_cparams(("parallel", "parallel")),
        name="mlstm",
    )(bg, p_lat, p_lat, p_lat, p_lat, gt_lat, p_ctx, p_ctx, gt_ctx, cw, cw, cb, cb, gn)


def _attn_kernel(q_ref, k_ref, vt_ref, kc_ref, vtc_ref, lam_ref, gd_ref, o_ref,
                 qq_ref, m_ref, acc_ref, s0_ref, s1_ref, cm0_ref, cm1_ref,
                 p0_ref, p1_ref, al0_ref, al1_ref, *, lam_init):
    tq = q_ref.shape[0]
    n_kv = k_ref.shape[0] // ATT_TK

    lane = lax.broadcasted_iota(jnp.int32, (tq, HEAD_DIM), 1)
    q = q_ref[...]
    zero = jnp.zeros_like(q)
    qq_ref[0:tq, :] = jnp.where(lane < DIFF_QK_DIM, q, zero)
    qq_ref[tq:2 * tq, :] = jnp.where(lane >= DIFF_QK_DIM, q, zero)

    sbufs = ((s0_ref, cm0_ref), (s1_ref, cm1_ref))
    pbufs = ((p0_ref, al0_ref), (p1_ref, al1_ref))
    strips = [slice(c * ATT_STRIP, (c + 1) * ATT_STRIP) for c in range(2 * tq // ATT_STRIP)]

    def k_lat(t):
        return k_ref[pl.ds(pl.multiple_of(t * ATT_TK, ATT_TK), ATT_TK), :]

    def with_ones(vt):
        return jnp.concatenate([vt, jnp.ones((BF16_SUBLANES, vt.shape[1]), BF16)], axis=0)

    k_abs = jnp.maximum(jnp.max(jnp.abs(k_ref[...]), axis=0, keepdims=True),
                        jnp.max(jnp.abs(kc_ref[...]), axis=0, keepdims=True))
    bound = _dot_nt(jnp.broadcast_to(k_abs, (BF16_SUBLANES, HEAD_DIM)), jnp.abs(qq_ref[...]))
    m_ref[...] = bound[0:1, :] - SHIFT_HEADROOM_LOG2
    acc_ref[...] = jnp.zeros(acc_ref.shape, F32)

    def shifted_weights(k, buf, cs):
        st = _dot_nt(k, qq_ref[cs, :])
        pbufs[buf][0][0:k.shape[0], cs] = jnp.exp2(st - m_ref[:, cs]).astype(BF16)

    def accumulate(vt_aug, buf, cs):
        acc_ref[:, cs] += _dot(vt_aug, pbufs[buf][0][0:vt_aug.shape[1], cs])

    def fixed_substep(k, w_buf, vt, pv_buf):
        if vt is not None:
            vt_aug = with_ones(vt)
        for cs in strips:
            if k is not None:
                shifted_weights(k, w_buf, cs)
            if vt is not None:
                accumulate(vt_aug, pv_buf, cs)

    fixed_substep(kc_ref[...], 0, None, None)
    fixed_substep(k_lat(0), 1, vtc_ref[0], 0)

    def fixed_body(i, carry):
        t = 2 * i
        fixed_substep(k_lat(t + 1), 0, vt_ref[t], 1)
        fixed_substep(k_lat(t + 2), 1, vt_ref[t + 1], 0)
        return carry

    lax.fori_loop(0, n_kv // 2 - 1, fixed_body, 0)
    fixed_substep(k_lat(n_kv - 1), 0, vt_ref[n_kv - 2], 1)
    fixed_substep(None, None, vt_ref[n_kv - 1], 0)

    l_min = jnp.min(acc_ref[HEAD_DIM:HEAD_DIM + 1, :])
    in_range = l_min >= 2.0 ** -SHIFT_HEADROOM_LOG2

    def score(k, buf, cs):
        s_ref, cm_ref = sbufs[buf]
        st = _dot_nt(k, qq_ref[cs, :])
        s_ref[0:k.shape[0], cs] = st
        cm_ref[:, cs] = jnp.max(st, axis=0, keepdims=True)

    def softmax(buf, rows, cs):
        s_ref, cm_ref = sbufs[buf]
        p_ref, al_ref = pbufs[buf]
        m_prev = m_ref[:, cs]
        m_new = jnp.maximum(m_prev, cm_ref[:, cs])
        p_ref[0:rows, cs] = jnp.exp2((s_ref[0:rows, cs] - m_new).astype(BF16))
        al_ref[:, cs] = jnp.exp2(m_prev - m_new)
        m_ref[:, cs] = m_new

    def pv(vt_aug, buf, cs):
        p_ref, al_ref = pbufs[buf]
        rows = vt_aug.shape[1]
        acc_ref[:, cs] = al_ref[:, cs] * acc_ref[:, cs] + _dot(vt_aug, p_ref[0:rows, cs])

    def substep(k, score_buf, soft_buf, soft_rows, vt, pv_buf):
        if vt is not None:
            vt_aug = with_ones(vt)
        for cs in strips:
            if k is not None:
                score(k, score_buf, cs)
            if soft_buf is not None:
                softmax(soft_buf, soft_rows, cs)
            if vt is not None:
                pv(vt_aug, pv_buf, cs)

    def body(i, carry):
        t = 2 * i
        substep(k_lat(t + 2), 1, 0, ATT_TK, vt_ref[t], 1)
        substep(k_lat(t + 3), 0, 1, ATT_TK, vt_ref[t + 1], 0)
        return carry

    @pl.when(jnp.logical_not(in_range))
    def _():
        m_ref[...] = jnp.full(m_ref.shape, -jnp.inf, F32)
        acc_ref[...] = jnp.zeros(acc_ref.shape, F32)
        substep(kc_ref[...], 0, None, None, None, None)
        substep(k_lat(0), 1, 0, CTX_LEN, None, None)
        substep(k_lat(1), 0, 1, ATT_TK, vtc_ref[0], 0)
        lax.fori_loop(0, n_kv // 2 - 1, body, 0)
        substep(None, None, 0, ATT_TK, vt_ref[n_kv - 2], 1)
        substep(None, None, None, None, vt_ref[n_kv - 1], 0)

    lp = lam_ref[...]
    e1 = jnp.exp(jnp.sum(lp[0:1, :] * lp[1:2, :], axis=1, keepdims=True))
    e2 = jnp.exp(jnp.sum(lp[2:3, :] * lp[3:4, :], axis=1, keepdims=True))
    lam = e1 - e2 + lam_init
    o = acc_ref[0:HEAD_DIM, :] / acc_ref[HEAD_DIM:HEAD_DIM + 1, :]
    od = o[:, 0:tq] - lam * o[:, tq:2 * tq]
    od = od * lax.rsqrt(jnp.mean(od * od, axis=0, keepdims=True) + EPS)
    o_ref[...] = (od.T * gd_ref[...] * (1.0 - lam_init)).astype(BF16)


def _attn_call(p_lat, vt_lat, p_ctx, vt_ctx, lamp, gd, *, batch, seq, lam_init):
    nq = seq // ATT_TQ
    n_kv = seq // ATT_TK
    kern = functools.partial(_attn_kernel, lam_init=lam_init)
    return pl.pallas_call(
        kern,
        grid=(batch, HEADS, nq),
        in_specs=[
            pl.BlockSpec((ATT_TQ, HEAD_DIM), lambda b, h, i: (b * nq + i, TILE_DQ * HEADS + h)),
            pl.BlockSpec((seq, HEAD_DIM), lambda b, h, i: (b, TILE_DK * HEADS + h)),
            pl.BlockSpec((n_kv, HEAD_DIM, ATT_TK), lambda b, h, i: (b, h, 0)),
            pl.BlockSpec((CTX_LEN, HEAD_DIM), lambda b, h, i: (b, TILE_DK * HEADS + h)),
            pl.BlockSpec((1, HEAD_DIM, CTX_LEN), lambda b, h, i: (b, h, 0)),
            pl.BlockSpec((8, LANES), lambda b, h, i: (0, 0)),
            pl.BlockSpec((1, HEAD_DIM), lambda b, h, i: (0, 0)),
        ],
        out_specs=pl.BlockSpec((ATT_TQ, HEAD_DIM), lambda b, h, i: (b * nq + i, h)),
        out_shape=jax.ShapeDtypeStruct((batch * seq, GROUP_WIDTH), BF16),
        scratch_shapes=[
            pltpu.VMEM((2 * ATT_TQ, HEAD_DIM), BF16),
            pltpu.VMEM((1, 2 * ATT_TQ), F32),
            pltpu.VMEM((HEAD_DIM + BF16_SUBLANES, 2 * ATT_TQ), F32),
            pltpu.VMEM((ATT_TK, 2 * ATT_TQ), F32),
            pltpu.VMEM((ATT_TK, 2 * ATT_TQ), F32),
            pltpu.VMEM((1, 2 * ATT_TQ), F32),
            pltpu.VMEM((1, 2 * ATT_TQ), F32),
            pltpu.VMEM((ATT_TK, 2 * ATT_TQ), BF16),
            pltpu.VMEM((ATT_TK, 2 * ATT_TQ), BF16),
            pltpu.VMEM((1, 2 * ATT_TQ), F32),
            pltpu.VMEM((1, 2 * ATT_TQ), F32),
        ],
        compiler_params=_cparams(("parallel", "parallel", "arbitrary")),
        name="diffattn",
    )(p_lat, p_lat, vt_lat, p_ctx, vt_ctx, lamp, gd)


def _outproj_kernel(m_ref, d_ref, wa_ref, wb_ref, x_ref, ga_ref, sh_ref, sc_ref, gpost_ref, gpre_ref,
                    x1_ref, h_ref):
    y = _dot(m_ref[...], wa_ref[...]) + _dot(d_ref[...], wb_ref[...])
    x1 = x_ref[...] + ga_ref[0] * _rms(y, gpost_ref[...])
    x1_ref[...] = x1
    hh = _rms(x1, gpre_ref[...]) * (1.0 + sc_ref[0]) + sh_ref[0]
    h_ref[...] = hh.astype(BF16)


def _outproj_call(m_lat, d_lat, wa, wb, x2d, ga1, sh2, sc2, gpost, gpre, *, tm, seq):
    m = x2d.shape[0]
    bpb = seq // tm
    row = lambda i: (i, 0)
    fixed = lambda i: (0, 0)
    per_batch = lambda i: (i // bpb, 0, 0)
    return pl.pallas_call(
        _outproj_kernel,
        grid=(m // tm,),
        in_specs=[
            pl.BlockSpec((tm, GROUP_WIDTH), row),
            pl.BlockSpec((tm, GROUP_WIDTH), row),
            pl.BlockSpec((GROUP_WIDTH, D_MODEL), fixed),
            pl.BlockSpec((GROUP_WIDTH, D_MODEL), fixed),
            pl.BlockSpec((tm, D_MODEL), row),
            pl.BlockSpec((1, 1, D_MODEL), per_batch),
            pl.BlockSpec((1, 1, D_MODEL), per_batch),
            pl.BlockSpec((1, 1, D_MODEL), per_batch),
            pl.BlockSpec((1, D_MODEL), fixed),
            pl.BlockSpec((1, D_MODEL), fixed),
        ],
        out_specs=[pl.BlockSpec((tm, D_MODEL), row), pl.BlockSpec((tm, D_MODEL), row)],
        out_shape=[jax.ShapeDtypeStruct((m, D_MODEL), F32), jax.ShapeDtypeStruct((m, D_MODEL), BF16)],
        compiler_params=_cparams(("parallel",)),
        name="outproj",
    )(m_lat, d_lat, wa, wb, x2d, ga1, sh2, sc2, gpost, gpre)


def _ffn_kernel(h_ref, hp_ref, hn_ref, wg_ref, wv_ref, cwg_ref, cwv_ref, cbg_ref, cbv_ref, wd_ref,
                x1_ref, ga_ref, gpost_ref, o_ref, lhs_ref, acc_ref, ug_ref, uv_ref, a_ref, *, blocks_per_seq):
    i = pl.program_id(0)
    f = pl.program_id(1)
    tm = h_ref.shape[0]
    halo = BF16_SUBLANES

    @pl.when(f == 0)
    def _():
        lhs_ref[0:halo, :] = hp_ref[...]
        lhs_ref[halo:halo + tm, :] = h_ref[...]
        lhs_ref[halo + tm:2 * halo + tm, :] = hn_ref[...]
        acc_ref[...] = jnp.zeros(acc_ref.shape, F32)

    has_prev = ((i % blocks_per_seq) > 0).astype(F32)
    has_next = ((i % blocks_per_seq) < blocks_per_seq - 1).astype(F32)

    def up_project(w_ref, u_ref):
        u_ref[...] = _dot(lhs_ref[...], w_ref[...])
        u_ref[halo - 1:halo, :] = u_ref[halo - 1:halo, :] * has_prev
        u_ref[halo + tm:halo + tm + 1, :] = u_ref[halo + tm:halo + tm + 1, :] * has_next

    up_project(wg_ref, ug_ref)
    up_project(wv_ref, uv_ref)

    def conv(u_ref, r, cw, cb):
        lo = halo + r
        return (u_ref[lo - 1:lo - 1 + FFN_ROWS, :] * cw[0:1, :] + u_ref[lo:lo + FFN_ROWS, :] * cw[1:2, :]
                + u_ref[lo + 1:lo + 1 + FFN_ROWS, :] * cw[2:3, :] + cb)

    cwg, cwv, cbg, cbv = cwg_ref[...], cwv_ref[...], cbg_ref[...], cbv_ref[...]
    half = tm // 2
    for r0 in (0, half):
        for r in range(r0, r0 + half, FFN_ROWS):
            gate = conv(ug_ref, r, cwg, cbg)
            val = conv(uv_ref, r, cwv, cbv)
            a_ref[r:r + FFN_ROWS, :] = (gate * jax.nn.sigmoid(gate) * val).astype(BF16)
        acc_ref[r0:r0 + half, :] += _dot(a_ref[r0:r0 + half, :], wd_ref[...])

    @pl.when(f == pl.num_programs(1) - 1)
    def _():
        o_ref[...] = x1_ref[...] + ga_ref[0] * _rms(acc_ref[...], gpost_ref[...])


def _ffn_call(hff, w_up, cw, cb, w_down, x1, ga2, gpost, *, tm, tf, seq):
    m = hff.shape[0]
    nf = D_FF // tf
    bps = seq // tm
    halo = BF16_SUBLANES
    hb = tm // halo
    n_halo = m // halo
    kern = functools.partial(_ffn_kernel, blocks_per_seq=bps)
    row = lambda i, f: (i, 0)
    fixed = lambda i, f: (0, 0)
    return pl.pallas_call(
        kern,
        grid=(m // tm, nf),
        in_specs=[
            pl.BlockSpec((tm, D_MODEL), row),
            pl.BlockSpec((halo, D_MODEL), lambda i, f: (jnp.maximum(i * hb - 1, 0), 0)),
            pl.BlockSpec((halo, D_MODEL), lambda i, f: (jnp.minimum((i + 1) * hb, n_halo - 1), 0)),
            pl.BlockSpec((D_MODEL, tf), lambda i, f: (0, f)),
            pl.BlockSpec((D_MODEL, tf), lambda i, f: (0, nf + f)),
            pl.BlockSpec((3, tf), lambda i, f: (0, f)),
            pl.BlockSpec((3, tf), lambda i, f: (0, nf + f)),
            pl.BlockSpec((1, tf), lambda i, f: (0, f)),
            pl.BlockSpec((1, tf), lambda i, f: (0, nf + f)),
            pl.BlockSpec((tf, D_MODEL), lambda i, f: (f, 0)),
            pl.BlockSpec((tm, D_MODEL), row),
            pl.BlockSpec((1, 1, D_MODEL), lambda i, f: (i // bps, 0, 0)),
            pl.BlockSpec((1, D_MODEL), fixed),
        ],
        out_specs=pl.BlockSpec((tm, D_MODEL), row),
        out_shape=jax.ShapeDtypeStruct((m, D_MODEL), F32),
        scratch_shapes=[pltpu.VMEM((tm + 2 * halo, D_MODEL), BF16), pltpu.VMEM((tm, D_MODEL), F32),
                        pltpu.VMEM((tm + 2 * halo, tf), F32), pltpu.VMEM((tm + 2 * halo, tf), F32),
                        pltpu.VMEM((tm, tf), BF16)],
        compiler_params=_cparams(("parallel", "arbitrary")),
        name="convffn",
    )(hff, hff, hff, w_up, w_up, cw, cw, cb, cb, w_down, x1, ga2, gpost)


def _rope_tables(seq):
    rows = seq // GRID_W
    r = np.repeat(np.arange(rows, dtype=np.float64), GRID_W)
    c = np.tile(np.arange(GRID_W, dtype=np.float64), rows)
    inv_freq = ROPE_BASE ** (-np.arange(ROPE_FREQS, dtype=np.float64) / ROPE_FREQS)
    ar = r[:, None] * inv_freq
    ac = c[:, None] * inv_freq
    ang = np.concatenate([ar, ar, ac, ac, ar, ar, ac, ac], axis=1)
    sign = np.tile(np.concatenate([-np.ones(ROPE_FREQS), np.ones(ROPE_FREQS)]), 4)
    return jnp.asarray(np.cos(ang), F32), jnp.asarray(np.sin(ang) * sign, F32)


def kernel(x, c, ctx, c_ctx, w_mod, b_mod, g_pre_mix, g_post_mix, w_in, b_gate, conv_qk_w, conv_qk_b,
           g_mlstm, lambda_q1, lambda_k1, lambda_q2, lambda_k2, g_diff, w_out, g_pre_ffn, g_post_ffn,
           w_up, conv_ffn_w, conv_ffn_b, w_down):
    batch, seq, d = x.shape
    depth = w_mod.shape[0]
    assert depth == 1 and d == D_MODEL and ctx.shape[1] == CTX_LEN
    l = 0
    lam_init = 0.8 - 0.6 * math.exp(-0.3 * l)

    cc = jnp.zeros((8, d), F32).at[:batch].set(c).at[batch].set(c_ctx)
    mod = _mod_call(cc, w_mod[l], b_mod[l][None, :])
    mod = mod.reshape(8, N_MOD, 1, d)
    sh1, sc1, ga1, sh2, sc2, ga2 = (mod[:batch, k] for k in range(N_MOD))
    csh1 = jnp.broadcast_to(mod[batch, 0][None], (batch, 1, d))
    csc1 = jnp.broadcast_to(mod[batch, 1][None], (batch, 1, d))

    wl = w_in[l]
    wp = jnp.concatenate([wl[:, :OFF_MG], wl[:, OFF_DQ:]], axis=1).astype(BF16)
    wgt = jnp.pad(wl[:, OFF_MG:OFF_DQ], ((0, 0), (0, LANES - N_GATES))).astype(BF16)

    cos_t, sin_t = _rope_tables(seq)
    g_pre = g_pre_mix[l][None, :]
    x2d = x.reshape(batch * seq, d)
    ctx2d = ctx.reshape(batch * CTX_LEN, d)

    p_lat, vt_lat, gt_lat = _inproj_call(x2d, g_pre, sh1, sc1, wp, wgt, cos_t, sin_t,
                                         tm=512, rows_per_batch=seq, rope=True, tk_v=ATT_TK)
    p_ctx, vt_ctx, gt_ctx = _inproj_call(ctx2d, g_pre, csh1, csc1, wp, wgt, cos_t, sin_t,
                                         tm=CTX_LEN, rows_per_batch=CTX_LEN, rope=False, tk_v=CTX_LEN)

    m_lat = _mlstm_call(b_gate[l], p_lat, gt_lat, p_ctx, gt_ctx, conv_qk_w[l], conv_qk_b[l][None, :],
                        g_mlstm[l][None, :], batch=batch, seq=seq)

    lamp = jnp.zeros((8, LANES), F32)
    lamp = lamp.at[0, :DIFF_QK_DIM].set(lambda_q1[l]).at[1, :DIFF_QK_DIM].set(lambda_k1[l])
    lamp = lamp.at[2, :DIFF_QK_DIM].set(lambda_q2[l]).at[3, :DIFF_QK_DIM].set(lambda_k2[l])
    d_lat = _attn_call(p_lat, vt_lat, p_ctx, vt_ctx, lamp, g_diff[l][None, :],
                       batch=batch, seq=seq, lam_init=lam_init)

    wo = w_out[l].astype(BF16)
    x1, hff = _outproj_call(m_lat, d_lat, wo[:GROUP_WIDTH], wo[GROUP_WIDTH:], x2d, ga1, sh2, sc2,
                            g_post_mix[l][None, :], g_pre_ffn[l][None, :], tm=512, seq=seq)

    x2 = _ffn_call(hff, w_up[l].astype(BF16), conv_ffn_w[l], conv_ffn_b[l][None, :],
                   w_down[l].astype(BF16), x1, ga2, g_post_ffn[l][None, :], tm=512, tf=512, seq=seq)
    return x2.reshape(batch, seq, d)
```

```python
import functools
import math

import jax
import jax.numpy as jnp
import numpy as np
from jax import lax
from jax.experimental import pallas as pl
from jax.experimental.pallas import tpu as pltpu

F32 = jnp.float32
BF16 = jnp.bfloat16

D_MODEL = 2048
CTX_LEN = 256
GRID_W = 64
HEADS = 8
HEAD_DIM = 128
GROUP_WIDTH = HEADS * HEAD_DIM
CHUNK = 128
DIFF_QK_DIM = 64
ROPE_FREQS = 16
ROPE_BASE = 10000.0
D_FF = 5632
N_MOD = 6
EPS = 1e-6
N_GATES = 4 * HEADS

OFF_MG = 4 * GROUP_WIDTH
OFF_DQ = OFF_MG + N_GATES

LANES = 128
BF16_SUBLANES = 16
MXU_COLS = 256
VMEM_LIMIT = 56 * 1024 * 1024

TILE_MQ, TILE_MK, TILE_MV, TILE_MO, TILE_DQ, TILE_DK, TILE_DV = range(7)
N_PROJ_TILES = 7
P_TILES = 6

ATT_TQ = 2048
ATT_TK = 512
ATT_STRIP = 256
FFN_ROWS = 64
SHIFT_HEADROOM_LOG2 = 60.0
QK_SCALE_LOG2E = DIFF_QK_DIM ** -0.5 * 1.4426950408889634


def _cparams(sem, flags=None):
    return pltpu.CompilerParams(dimension_semantics=sem, vmem_limit_bytes=VMEM_LIMIT, flags=flags)


def _dot(a, b):
    return jnp.dot(a, b, preferred_element_type=F32)


def _dot_nt(a, b):
    return lax.dot_general(a, b, (((1,), (1,)), ((), ())), preferred_element_type=F32)


def _rms(y, g):
    return y * lax.rsqrt(jnp.mean(y * y, axis=-1, keepdims=True) + EPS) * g


def _mod_kernel(c_ref, w_ref, b_ref, o_ref):
    cc = c_ref[...]
    s = cc * jax.nn.sigmoid(cc)
    o_ref[...] = _dot(s.astype(BF16), w_ref[...].astype(BF16)) + b_ref[...]


def _mod_call(cc, w_mod, b_mod):
    n = w_mod.shape[1]
    tn = 1024
    return pl.pallas_call(
        _mod_kernel,
        grid=(n // tn,),
        in_specs=[pl.BlockSpec((8, D_MODEL), lambda j: (0, 0)),
                  pl.BlockSpec((D_MODEL, tn), lambda j: (0, j)),
                  pl.BlockSpec((1, tn), lambda j: (0, j))],
        out_specs=pl.BlockSpec((8, tn), lambda j: (0, j)),
        out_shape=jax.ShapeDtypeStruct((8, n), F32),
        compiler_params=_cparams(("parallel",)),
        name="mod",
    )(cc, w_mod, b_mod)


def _rope(acc, cos, sin_signed, first_half):
    out = []
    for c in range(acc.shape[1] // LANES):
        a = acc[:, c * LANES:(c + 1) * LANES]
        rot = jnp.where(first_half, pltpu.roll(a, LANES - ROPE_FREQS, 1), pltpu.roll(a, ROPE_FREQS, 1))
        out.append(a * cos + rot * sin_signed)
    return jnp.concatenate(out, axis=1)


def _inproj_kernel(x_ref, g_ref, sh_ref, sc_ref, w_ref, wgt_ref, cos_ref, sin_ref,
                   p_ref, vt_ref, gt_ref, xn_ref, *, rope, tk_v):
    j = pl.program_id(1)
    tm = x_ref.shape[0]

    @pl.when(j == 0)
    def _():
        x = x_ref[...]
        y = _rms(x, g_ref[...]) * (1.0 + sc_ref[0]) + sh_ref[0]
        xn = y.astype(BF16)
        xn_ref[...] = xn
        gt = _dot(xn, wgt_ref[...]).T[0:N_GATES, :]
        for c in range(tm // CHUNK):
            gt_ref[c] = gt[:, c * CHUNK:(c + 1) * CHUNK]

    acc = _dot(xn_ref[...], w_ref[...])

    is_rope = (j == TILE_DQ) | (j == TILE_DK)
    is_v = j == TILE_DV

    if rope:
        @pl.when(is_rope)
        def _():
            lane = lax.broadcasted_iota(jnp.int32, (tm, LANES), 1)
            first_half = (lane % (2 * ROPE_FREQS)) < ROPE_FREQS
            fac = jnp.where(j == TILE_DQ, QK_SCALE_LOG2E, 1.0)
            p_ref[...] = (_rope(acc, cos_ref[...], sin_ref[...], first_half) * fac).astype(BF16)

        plain = jnp.logical_not(is_rope | is_v)
    else:
        plain = jnp.logical_not(is_v)

    @pl.when(plain)
    def _():
        p_ref[...] = acc.astype(BF16)

    @pl.when(is_v)
    def _():
        at = acc.T.astype(BF16)
        for c in range(tm // tk_v):
            vt_ref[c] = at[:, c * tk_v:(c + 1) * tk_v]


def _inproj_call(x2d, g, sh, sc, wp, wgt, cos_t, sin_t, *, tm, rows_per_batch, rope, tk_v):
    m = x2d.shape[0]
    blocks_per_batch = max(rows_per_batch // tm, 1)
    kern = functools.partial(_inproj_kernel, rope=rope, tk_v=tk_v)
    return pl.pallas_call(
        kern,
        grid=(m // tm, N_PROJ_TILES),
        in_specs=[
            pl.BlockSpec((tm, D_MODEL), lambda i, j: (i, 0)),
            pl.BlockSpec((1, D_MODEL), lambda i, j: (0, 0)),
            pl.BlockSpec((1, 1, D_MODEL), lambda i, j: (i // blocks_per_batch, 0, 0)),
            pl.BlockSpec((1, 1, D_MODEL), lambda i, j: (i // blocks_per_batch, 0, 0)),
            pl.BlockSpec((D_MODEL, GROUP_WIDTH), lambda i, j: (0, j)),
            pl.BlockSpec((D_MODEL, LANES), lambda i, j: (0, 0)),
            pl.BlockSpec((tm, LANES), lambda i, j: (i % blocks_per_batch, 0)),
            pl.BlockSpec((tm, LANES), lambda i, j: (i % blocks_per_batch, 0)),
        ],
        out_specs=[
            pl.BlockSpec((tm, GROUP_WIDTH), lambda i, j: (i, jnp.minimum(j, P_TILES - 1))),
            pl.BlockSpec((tm // tk_v, GROUP_WIDTH, tk_v), lambda i, j: (i, 0, 0)),
            pl.BlockSpec((tm // CHUNK, N_GATES, CHUNK), lambda i, j: (i, 0, 0)),
        ],
        out_shape=[
            jax.ShapeDtypeStruct((m, P_TILES * GROUP_WIDTH), BF16),
            jax.ShapeDtypeStruct((m // tk_v, GROUP_WIDTH, tk_v), BF16),
            jax.ShapeDtypeStruct((m // CHUNK, N_GATES, CHUNK), F32),
        ],
        scratch_shapes=[pltpu.VMEM((tm, D_MODEL), BF16)],
        compiler_params=_cparams(("parallel", "arbitrary")),
        name="inproj_rope" if rope else "inproj_ctx",
    )(x2d, g, sh, sc, wp, wgt, cos_t, sin_t)


def _log_sigmoid(x):
    return jnp.minimum(x, 0.0) - jnp.log1p(jnp.exp(-jnp.abs(x)))


def _split_dot(row, tri):
    hi = row.astype(BF16)
    lo = (row - hi.astype(F32)).astype(BF16)
    hi8 = jnp.broadcast_to(hi, (BF16_SUBLANES, CHUNK))
    lo8 = jnp.broadcast_to(lo, (BF16_SUBLANES, CHUNK))
    return (_dot(hi8, tri) + _dot(lo8, tri))[0:1, :]


def _mlstm_kernel(bg_ref,
                  q_ref, k_ref, v_ref, o_ref, gt_ref,
                  kc_ref, vc_ref, gtc_ref,
                  cwq_ref, cwk_ref, cbq_ref, cbk_ref, gn_ref,
                  out_ref,
                  qs_ref, kst_ref, gs_ref, kstc_ref, gsc_ref, hf_ref, hb_ref, cst_ref):
    h = pl.program_id(1)
    n_lat = q_ref.shape[0] // CHUNK
    n_ctx = kc_ref.shape[0] // CHUNK

    row_i = lax.broadcasted_iota(jnp.int32, (CHUNK, CHUNK), 0)
    col_i = lax.broadcasted_iota(jnp.int32, (CHUNK, CHUNK), 1)
    tril = col_i <= row_i
    triu = col_i >= row_i
    tril_b = tril.astype(BF16)
    triu_b = triu.astype(BF16)
    ones_col = (col_i == 0).astype(BF16)
    first_row = row_i == 0
    last_row = row_i == CHUNK - 1

    def conv_silu(ref, c, n_chunks, w_ref, b_ref):
        r0 = pl.multiple_of(c * CHUNK, CHUNK)
        main = ref[pl.ds(r0, CHUNK), :].astype(F32)
        pstart = pl.multiple_of(jnp.maximum(r0 - BF16_SUBLANES, 0), BF16_SUBLANES)
        nstart = pl.multiple_of(jnp.minimum(r0 + CHUNK, (n_chunks - 1) * CHUNK), BF16_SUBLANES)
        prev_blk = ref[pl.ds(pstart, BF16_SUBLANES), :].astype(F32)
        next_blk = ref[pl.ds(nstart, BF16_SUBLANES), :].astype(F32)
        prev_row = prev_blk[BF16_SUBLANES - 1:BF16_SUBLANES, :] * jnp.where(c > 0, 1.0, 0.0)
        next_row = next_blk[0:1, :] * jnp.where(c < n_chunks - 1, 1.0, 0.0)
        up = jnp.where(first_row, prev_row, pltpu.roll(main, 1, 0))
        dn = jnp.where(last_row, next_row, pltpu.roll(main, CHUNK - 1, 0))
        w = w_ref[...]
        u = up * w[0:1, :] + main * w[1:2, :] + dn * w[2:3, :] + b_ref[...]
        return u * jax.nn.sigmoid(u)

    def gate_rows(g_ref3, dst_ref, c):
        def row(t):
            return g_ref3[c, pl.ds(t * HEADS + h, 1), :] + bg_ref[t, h]
        i_f, f_f, i_b, f_b = row(0), row(1), row(2), row(3)
        lf_f = _log_sigmoid(f_f)
        lf_b = _log_sigmoid(f_b)
        dst_ref[c, 0:1, :] = i_f - _split_dot(lf_f, triu_b)
        dst_ref[c, 1:2, :] = lf_f
        dst_ref[c, 2:3, :] = i_b - _split_dot(lf_b, tril_b)
        dst_ref[c, 3:4, :] = lf_b

    k_scale = HEAD_DIM ** -0.5

    def prep_lat(c, carry):
        qs_ref[c] = conv_silu(q_ref, c, n_lat, cwq_ref, cbq_ref).astype(BF16)
        kk = conv_silu(k_ref, c, n_lat, cwk_ref, cbk_ref) * k_scale
        kst_ref[c] = kk.T.astype(BF16)
        gate_rows(gt_ref, gs_ref, c)
        return carry

    lax.fori_loop(0, n_lat, prep_lat, 0, unroll=2)

    def prep_ctx(c, carry):
        kk = conv_silu(kc_ref, c, n_ctx, cwk_ref, cbk_ref) * k_scale
        kstc_ref[c] = kk.T.astype(BF16)
        gate_rows(gtc_ref, gsc_ref, c)
        return carry

    lax.fori_loop(0, n_ctx, prep_ctx, 0)

    def vaug(vref, c):
        r0 = pl.multiple_of(c * CHUNK, CHUNK)
        return jnp.concatenate([vref[pl.ds(r0, CHUNK), :], ones_col], axis=1)

    def state_update(d, m_prev, a_row, lf_row, kt, va):
        m_last = jnp.maximum(m_prev, jnp.max(a_row, axis=1, keepdims=True))
        b_last = jnp.sum(lf_row, axis=1, keepdims=True)
        w_row = jnp.exp(a_row - m_last)
        decay = jnp.exp(m_prev - m_last)
        kw = (kt.astype(F32) * w_row).astype(BF16)
        cst_ref[d] = decay * cst_ref[d] + _dot(kw, va)
        return b_last + m_last

    def chunk_step(d, c, m_prev, mask):
        a_row = gs_ref[c, 2 * d:2 * d + 1, :]
        lf_row = gs_ref[c, 2 * d + 1:2 * d + 2, :]
        q = qs_ref[c]
        kt = kst_ref[c]
        va = vaug(v_ref, c)
        a_mat = jnp.where(mask, a_row, -jnp.inf)
        m_col = jnp.maximum(jnp.max(a_mat, axis=1, keepdims=True), m_prev)
        w_mat = jnp.exp(a_mat - m_col)
        w_inter = jnp.exp(m_prev - m_col)
        b_col = jnp.sum(jnp.where(mask, lf_row, 0.0), axis=1, keepdims=True)
        s = (_dot(q, kt) * w_mat).astype(BF16)
        nd = w_inter * _dot(q, cst_ref[d].astype(BF16)) + _dot(s, va)
        num = nd[:, :HEAD_DIM]
        den = nd[:, HEAD_DIM:HEAD_DIM + 1]
        hh = num / jnp.maximum(jnp.abs(den), jnp.exp(-b_col - m_col))
        m_new = state_update(d, m_prev, a_row, lf_row, kt, va)
        return hh, m_new

    cst_ref[...] = jnp.zeros(cst_ref.shape, F32)

    def ctx_fwd(c, m):
        return state_update(0, m, gsc_ref[c, 0:1, :], gsc_ref[c, 1:2, :], kstc_ref[c], vaug(vc_ref, c))

    def ctx_bwd(i, m):
        c = n_ctx - 1 - i
        return state_update(1, m, gsc_ref[c, 2:3, :], gsc_ref[c, 3:4, :], kstc_ref[c], vaug(vc_ref, c))

    m_f = lax.fori_loop(0, n_ctx, ctx_fwd, jnp.zeros((1, 1), F32))
    m_b = lax.fori_loop(0, n_ctx, ctx_bwd, jnp.zeros((1, 1), F32))

    def main_step(i, carry):
        m_f, m_b = carry
        hf, m_f = chunk_step(0, i, m_f, tril)
        hf_ref[i] = hf
        cb = n_lat - 1 - i
        hb, m_b = chunk_step(1, cb, m_b, triu)
        hb_ref[cb] = hb
        return m_f, m_b

    lax.fori_loop(0, n_lat, main_step, (m_f, m_b), unroll=8)

    def finish(c, carry):
        r0 = pl.multiple_of(c * CHUNK, CHUNK)
        hh = _rms(hf_ref[c] + hb_ref[c], gn_ref[...])
        og = o_ref[pl.ds(r0, CHUNK), :].astype(F32)
        out_ref[pl.ds(r0, CHUNK), :] = (hh * jax.nn.sigmoid(og)).astype(BF16)
        return carry

    lax.fori_loop(0, n_lat, finish, 0, unroll=4)


def _mlstm_call(bg, p_lat, gt_lat, p_ctx, gt_ctx, cw, cb, gn, *, batch, seq):
    n_lat = seq // CHUNK
    n_ctx = CTX_LEN // CHUNK

    def col(tile):
        return lambda b, h: (b, tile * HEADS + h)

    lat_blk = (seq, HEAD_DIM)
    ctx_blk = (CTX_LEN, HEAD_DIM)
    return pl.pallas_call(
        _mlstm_kernel,
        grid=(batch, HEADS),
        in_specs=[
            pl.BlockSpec(memory_space=pltpu.SMEM),
            pl.BlockSpec(lat_blk, col(TILE_MQ)),
            pl.BlockSpec(lat_blk, col(TILE_MK)),
            pl.BlockSpec(lat_blk, col(TILE_MV)),
            pl.BlockSpec(lat_blk, col(TILE_MO)),
            pl.BlockSpec((n_lat, N_GATES, CHUNK), lambda b, h: (b, 0, 0)),
            pl.BlockSpec(ctx_blk, col(TILE_MK)),
            pl.BlockSpec(ctx_blk, col(TILE_MV)),
            pl.BlockSpec((n_ctx, N_GATES, CHUNK), lambda b, h: (b, 0, 0)),
            pl.BlockSpec((3, HEAD_DIM), lambda b, h: (0, h)),
            pl.BlockSpec((3, HEAD_DIM), lambda b, h: (0, HEADS + h)),
            pl.BlockSpec((1, HEAD_DIM), lambda b, h: (0, h)),
            pl.BlockSpec((1, HEAD_DIM), lambda b, h: (0, HEADS + h)),
            pl.BlockSpec((1, HEAD_DIM), lambda b, h: (0, h)),
        ],
        out_specs=pl.BlockSpec(lat_blk, lambda b, h: (b, h)),
        out_shape=jax.ShapeDtypeStruct((batch * seq, GROUP_WIDTH), BF16),
        scratch_shapes=[
            pltpu.VMEM((n_lat, CHUNK, HEAD_DIM), BF16),
            pltpu.VMEM((n_lat, HEAD_DIM, CHUNK), BF16),
            pltpu.VMEM((n_lat, 8, CHUNK), F32),
            pltpu.VMEM((n_ctx, HEAD_DIM, CHUNK), BF16),
            pltpu.VMEM((n_ctx, 8, CHUNK), F32),
            pltpu.VMEM((n_lat, CHUNK, HEAD_DIM), F32),
            pltpu.VMEM((n_lat, CHUNK, HEAD_DIM), F32),
            pltpu.VMEM((2, HEAD_DIM, 2 * HEAD_DIM), F32),
        ],
        compiler_params=_cparams(("parallel", "parallel")),
        name="mlstm",
    )(bg, p_lat, p_lat, p_lat, p_lat, gt_lat, p_ctx, p_ctx, gt_ctx, cw, cw, cb, cb, gn)


def _attn_kernel(q_ref, k_ref, vt_ref, kc_ref, vtc_ref, lam_ref, gd_ref, o_ref,
                 qq_ref, m_ref, acc_ref, s0_ref, s1_ref, cm0_ref, cm1_ref,
                 p0_ref, p1_ref, al0_ref, al1_ref, l_ref, *, lam_init):
    tq = q_ref.shape[0]
    n_kv = k_ref.shape[0] // ATT_TK

    lane = lax.broadcasted_iota(jnp.int32, (tq, HEAD_DIM), 1)
    q = q_ref[...]
    zero = jnp.zeros_like(q)
    qq_ref[0:tq, :] = jnp.where(lane < DIFF_QK_DIM, q, zero)
    qq_ref[tq:2 * tq, :] = jnp.where(lane >= DIFF_QK_DIM, q, zero)

    sbufs = ((s0_ref, cm0_ref), (s1_ref, cm1_ref))
    pbufs = ((p0_ref, al0_ref), (p1_ref, al1_ref))
    strips = [slice(c * ATT_STRIP, (c + 1) * ATT_STRIP) for c in range(2 * tq // ATT_STRIP)]

    def k_lat(t):
        return k_ref[pl.ds(pl.multiple_of(t * ATT_TK, ATT_TK), ATT_TK), :]

    def with_ones(vt):
        return jnp.concatenate([vt, jnp.ones((BF16_SUBLANES, vt.shape[1]), BF16)], axis=0)

    k_abs = jnp.maximum(jnp.max(jnp.abs(k_ref[...]), axis=0, keepdims=True),
                        jnp.max(jnp.abs(kc_ref[...]), axis=0, keepdims=True))
    bound = _dot_nt(jnp.broadcast_to(k_abs, (BF16_SUBLANES, HEAD_DIM)), jnp.abs(qq_ref[...]))
    m_ref[...] = bound[0:1, :] - SHIFT_HEADROOM_LOG2
    acc_ref[...] = jnp.zeros(acc_ref.shape, F32)

    l_ref[...] = jnp.zeros(l_ref.shape, F32)

    def shifted_weights(k, buf, cs):
        st = _dot_nt(k, qq_ref[cs, :])
        w = jnp.exp2(st - m_ref[:, cs])
        l_ref[:, cs] += jnp.sum(w, axis=0, keepdims=True)
        pbufs[buf][0][0:k.shape[0], cs] = w.astype(BF16)

    def accumulate(vt, buf, cs):
        acc_ref[0:HEAD_DIM, cs] += _dot(vt, pbufs[buf][0][0:vt.shape[1], cs])

    def fixed_substep(k, w_buf, vt, pv_buf):
        for cs in strips:
            if k is not None:
                shifted_weights(k, w_buf, cs)
            if vt is not None:
                accumulate(vt, pv_buf, cs)

    fixed_substep(kc_ref[...], 0, None, None)
    fixed_substep(k_lat(0), 1, vtc_ref[0], 0)

    def fixed_body(i, carry):
        t = 2 * i
        fixed_substep(k_lat(t + 1), 0, vt_ref[t], 1)
        fixed_substep(k_lat(t + 2), 1, vt_ref[t + 1], 0)
        return carry

    lax.fori_loop(0, n_kv // 2 - 1, fixed_body, 0)
    fixed_substep(k_lat(n_kv - 1), 0, vt_ref[n_kv - 2], 1)
    fixed_substep(None, None, vt_ref[n_kv - 1], 0)

    l_min = jnp.min(l_ref[...])
    in_range = l_min >= 2.0 ** -SHIFT_HEADROOM_LOG2

    def score(k, buf, cs):
        s_ref, cm_ref = sbufs[buf]
        st = _dot_nt(k, qq_ref[cs, :])
        s_ref[0:k.shape[0], cs] = st
        cm_ref[:, cs] = jnp.max(st, axis=0, keepdims=True)

    def softmax(buf, rows, cs):
        s_ref, cm_ref = sbufs[buf]
        p_ref, al_ref = pbufs[buf]
        m_prev = m_ref[:, cs]
        m_new = jnp.maximum(m_prev, cm_ref[:, cs])
        p_ref[0:rows, cs] = jnp.exp2((s_ref[0:rows, cs] - m_new).astype(BF16))
        al_ref[:, cs] = jnp.exp2(m_prev - m_new)
        m_ref[:, cs] = m_new

    def pv(vt_aug, buf, cs):
        p_ref, al_ref = pbufs[buf]
        rows = vt_aug.shape[1]
        acc_ref[:, cs] = al_ref[:, cs] * acc_ref[:, cs] + _dot(vt_aug, p_ref[0:rows, cs])

    def substep(k, score_buf, soft_buf, soft_rows, vt, pv_buf):
        if vt is not None:
            vt_aug = with_ones(vt)
        for cs in strips:
            if k is not None:
                score(k, score_buf, cs)
            if soft_buf is not None:
                softmax(soft_buf, soft_rows, cs)
            if vt is not None:
                pv(vt_aug, pv_buf, cs)

    def body(i, carry):
        t = 2 * i
        substep(k_lat(t + 2), 1, 0, ATT_TK, vt_ref[t], 1)
        substep(k_lat(t + 3), 0, 1, ATT_TK, vt_ref[t + 1], 0)
        return carry

    @pl.when(jnp.logical_not(in_range))
    def _():
        m_ref[...] = jnp.full(m_ref.shape, -jnp.inf, F32)
        acc_ref[...] = jnp.zeros(acc_ref.shape, F32)
        substep(kc_ref[...], 0, None, None, None, None)
        substep(k_lat(0), 1, 0, CTX_LEN, None, None)
        substep(k_lat(1), 0, 1, ATT_TK, vtc_ref[0], 0)
        lax.fori_loop(0, n_kv // 2 - 1, body, 0)
        substep(None, None, 0, ATT_TK, vt_ref[n_kv - 2], 1)
        substep(None, None, None, None, vt_ref[n_kv - 1], 0)
        l_ref[...] = acc_ref[HEAD_DIM:HEAD_DIM + 1, :]

    lp = lam_ref[...]
    e1 = jnp.exp(jnp.sum(lp[0:1, :] * lp[1:2, :], axis=1, keepdims=True))
    e2 = jnp.exp(jnp.sum(lp[2:3, :] * lp[3:4, :], axis=1, keepdims=True))
    lam = e1 - e2 + lam_init
    o = acc_ref[0:HEAD_DIM, :] / l_ref[...]
    od = o[:, 0:tq] - lam * o[:, tq:2 * tq]
    od = od * lax.rsqrt(jnp.mean(od * od, axis=0, keepdims=True) + EPS)
    o_ref[...] = (od.T * gd_ref[...] * (1.0 - lam_init)).astype(BF16)


def _attn_call(p_lat, vt_lat, p_ctx, vt_ctx, lamp, gd, *, batch, seq, lam_init):
    nq = seq // ATT_TQ
    n_kv = seq // ATT_TK
    kern = functools.partial(_attn_kernel, lam_init=lam_init)
    return pl.pallas_call(
        kern,
        grid=(batch, HEADS, nq),
        in_specs=[
            pl.BlockSpec((ATT_TQ, HEAD_DIM), lambda b, h, i: (b * nq + i, TILE_DQ * HEADS + h)),
            pl.BlockSpec((seq, HEAD_DIM), lambda b, h, i: (b, TILE_DK * HEADS + h)),
            pl.BlockSpec((n_kv, HEAD_DIM, ATT_TK), lambda b, h, i: (b, h, 0)),
            pl.BlockSpec((CTX_LEN, HEAD_DIM), lambda b, h, i: (b, TILE_DK * HEADS + h)),
            pl.BlockSpec((1, HEAD_DIM, CTX_LEN), lambda b, h, i: (b, h, 0)),
            pl.BlockSpec((8, LANES), lambda b, h, i: (0, 0)),
            pl.BlockSpec((1, HEAD_DIM), lambda b, h, i: (0, 0)),
        ],
        out_specs=pl.BlockSpec((ATT_TQ, HEAD_DIM), lambda b, h, i: (b * nq + i, h)),
        out_shape=jax.ShapeDtypeStruct((batch * seq, GROUP_WIDTH), BF16),
        scratch_shapes=[
            pltpu.VMEM((2 * ATT_TQ, HEAD_DIM), BF16),
            pltpu.VMEM((1, 2 * ATT_TQ), F32),
            pltpu.VMEM((HEAD_DIM + BF16_SUBLANES, 2 * ATT_TQ), F32),
            pltpu.VMEM((ATT_TK, 2 * ATT_TQ), F32),
            pltpu.VMEM((ATT_TK, 2 * ATT_TQ), F32),
            pltpu.VMEM((1, 2 * ATT_TQ), F32),
            pltpu.VMEM((1, 2 * ATT_TQ), F32),
            pltpu.VMEM((ATT_TK, 2 * ATT_TQ), BF16),
            pltpu.VMEM((ATT_TK, 2 * ATT_TQ), BF16),
            pltpu.VMEM((1, 2 * ATT_TQ), F32),
            pltpu.VMEM((1, 2 * ATT_TQ), F32),
            pltpu.VMEM((1, 2 * ATT_TQ), F32),
        ],
        compiler_params=_cparams(("parallel", "parallel", "arbitrary")),
        name="diffattn",
    )(p_lat, p_lat, vt_lat, p_ctx, vt_ctx, lamp, gd)


def _outproj_kernel(m_ref, d_ref, wa_ref, wb_ref, x_ref, ga_ref, sh_ref, sc_ref, gpost_ref, gpre_ref,
                    x1_ref, h_ref):
    y = _dot(m_ref[...], wa_ref[...]) + _dot(d_ref[...], wb_ref[...])
    x1 = x_ref[...] + ga_ref[0] * _rms(y, gpost_ref[...])
    x1_ref[...] = x1
    hh = _rms(x1, gpre_ref[...]) * (1.0 + sc_ref[0]) + sh_ref[0]
    h_ref[...] = hh.astype(BF16)


def _outproj_call(m_lat, d_lat, wa, wb, x2d, ga1, sh2, sc2, gpost, gpre, *, tm, seq):
    m = x2d.shape[0]
    bpb = seq // tm
    row = lambda i: (i, 0)
    fixed = lambda i: (0, 0)
    per_batch = lambda i: (i // bpb, 0, 0)
    return pl.pallas_call(
        _outproj_kernel,
        grid=(m // tm,),
        in_specs=[
            pl.BlockSpec((tm, GROUP_WIDTH), row),
            pl.BlockSpec((tm, GROUP_WIDTH), row),
            pl.BlockSpec((GROUP_WIDTH, D_MODEL), fixed),
            pl.BlockSpec((GROUP_WIDTH, D_MODEL), fixed),
            pl.BlockSpec((tm, D_MODEL), row),
            pl.BlockSpec((1, 1, D_MODEL), per_batch),
            pl.BlockSpec((1, 1, D_MODEL), per_batch),
            pl.BlockSpec((1, 1, D_MODEL), per_batch),
            pl.BlockSpec((1, D_MODEL), fixed),
            pl.BlockSpec((1, D_MODEL), fixed),
        ],
        out_specs=[pl.BlockSpec((tm, D_MODEL), row), pl.BlockSpec((tm, D_MODEL), row)],
        out_shape=[jax.ShapeDtypeStruct((m, D_MODEL), F32), jax.ShapeDtypeStruct((m, D_MODEL), BF16)],
        compiler_params=_cparams(("parallel",)),
        name="outproj",
    )(m_lat, d_lat, wa, wb, x2d, ga1, sh2, sc2, gpost, gpre)


def _ffn_kernel(h_ref, hp_ref, hn_ref, wg_ref, wv_ref, cwg_ref, cwv_ref, cbg_ref, cbv_ref, wd_ref,
                x1_ref, ga_ref, gpost_ref, o_ref, lhs_ref, acc_ref, ug_ref, uv_ref, a_ref, *, blocks_per_seq):
    i = pl.program_id(0)
    f = pl.program_id(1)
    tm = h_ref.shape[0]
    halo = BF16_SUBLANES

    @pl.when(f == 0)
    def _():
        lhs_ref[0:halo, :] = hp_ref[...]
        lhs_ref[halo:halo + tm, :] = h_ref[...]
        lhs_ref[halo + tm:2 * halo + tm, :] = hn_ref[...]
        acc_ref[...] = jnp.zeros(acc_ref.shape, F32)

    has_prev = ((i % blocks_per_seq) > 0).astype(F32)
    has_next = ((i % blocks_per_seq) < blocks_per_seq - 1).astype(F32)

    def up_project(w_ref, u_ref):
        u_ref[...] = _dot(lhs_ref[...], w_ref[...])
        u_ref[halo - 1:halo, :] = u_ref[halo - 1:halo, :] * has_prev
        u_ref[halo + tm:halo + tm + 1, :] = u_ref[halo + tm:halo + tm + 1, :] * has_next

    up_project(wg_ref, ug_ref)
    up_project(wv_ref, uv_ref)

    def conv(u_ref, r, cw, cb):
        lo = halo + r
        return (u_ref[lo - 1:lo - 1 + FFN_ROWS, :] * cw[0:1, :] + u_ref[lo:lo + FFN_ROWS, :] * cw[1:2, :]
                + u_ref[lo + 1:lo + 1 + FFN_ROWS, :] * cw[2:3, :] + cb)

    cwg, cwv, cbg, cbv = cwg_ref[...], cwv_ref[...], cbg_ref[...], cbv_ref[...]
    half = tm // 2
    for r0 in (0, half):
        for r in range(r0, r0 + half, FFN_ROWS):
            gate = conv(ug_ref, r, cwg, cbg)
            val = conv(uv_ref, r, cwv, cbv)
            a_ref[r:r + FFN_ROWS, :] = (gate * jax.nn.sigmoid(gate) * val).astype(BF16)
        acc_ref[r0:r0 + half, :] += _dot(a_ref[r0:r0 + half, :], wd_ref[...])

    @pl.when(f == pl.num_programs(1) - 1)
    def _():
        o_ref[...] = x1_ref[...] + ga_ref[0] * _rms(acc_ref[...], gpost_ref[...])


def _ffn_call(hff, w_up, cw, cb, w_down, x1, ga2, gpost, *, tm, tf, seq):
    m = hff.shape[0]
    nf = D_FF // tf
    bps = seq // tm
    halo = BF16_SUBLANES
    hb = tm // halo
    n_halo = m // halo
    kern = functools.partial(_ffn_kernel, blocks_per_seq=bps)
    row = lambda i, f: (i, 0)
    fixed = lambda i, f: (0, 0)
    return pl.pallas_call(
        kern,
        grid=(m // tm, nf),
        in_specs=[
            pl.BlockSpec((tm, D_MODEL), row),
            pl.BlockSpec((halo, D_MODEL), lambda i, f: (jnp.maximum(i * hb - 1, 0), 0)),
            pl.BlockSpec((halo, D_MODEL), lambda i, f: (jnp.minimum((i + 1) * hb, n_halo - 1), 0)),
            pl.BlockSpec((D_MODEL, tf), lambda i, f: (0, f)),
            pl.BlockSpec((D_MODEL, tf), lambda i, f: (0, nf + f)),
            pl.BlockSpec((3, tf), lambda i, f: (0, f)),
            pl.BlockSpec((3, tf), lambda i, f: (0, nf + f)),
            pl.BlockSpec((1, tf), lambda i, f: (0, f)),
            pl.BlockSpec((1, tf), lambda i, f: (0, nf + f)),
            pl.BlockSpec((tf, D_MODEL), lambda i, f: (f, 0)),
            pl.BlockSpec((tm, D_MODEL), row),
            pl.BlockSpec((1, 1, D_MODEL), lambda i, f: (i // bps, 0, 0)),
            pl.BlockSpec((1, D_MODEL), fixed),
        ],
        out_specs=pl.BlockSpec((tm, D_MODEL), row),
        out_shape=jax.ShapeDtypeStruct((m, D_MODEL), F32),
        scratch_shapes=[pltpu.VMEM((tm + 2 * halo, D_MODEL), BF16), pltpu.VMEM((tm, D_MODEL), F32),
                        pltpu.VMEM((tm + 2 * halo, tf), F32), pltpu.VMEM((tm + 2 * halo, tf), F32),
                        pltpu.VMEM((tm, tf), BF16)],
        compiler_params=_cparams(("parallel", "arbitrary")),
        name="convffn",
    )(hff, hff, hff, w_up, w_up, cw, cw, cb, cb, w_down, x1, ga2, gpost)


def _rope_tables(seq):
    rows = seq // GRID_W
    r = np.repeat(np.arange(rows, dtype=np.float64), GRID_W)
    c = np.tile(np.arange(GRID_W, dtype=np.float64), rows)
    inv_freq = ROPE_BASE ** (-np.arange(ROPE_FREQS, dtype=np.float64) / ROPE_FREQS)
    ar = r[:, None] * inv_freq
    ac = c[:, None] * inv_freq
    ang = np.concatenate([ar, ar, ac, ac, ar, ar, ac, ac], axis=1)
    sign = np.tile(np.concatenate([-np.ones(ROPE_FREQS), np.ones(ROPE_FREQS)]), 4)
    return jnp.asarray(np.cos(ang), F32), jnp.asarray(np.sin(ang) * sign, F32)


def kernel(x, c, ctx, c_ctx, w_mod, b_mod, g_pre_mix, g_post_mix, w_in, b_gate, conv_qk_w, conv_qk_b,
           g_mlstm, lambda_q1, lambda_k1, lambda_q2, lambda_k2, g_diff, w_out, g_pre_ffn, g_post_ffn,
           w_up, conv_ffn_w, conv_ffn_b, w_down):
    batch, seq, d = x.shape
    depth = w_mod.shape[0]
    assert depth == 1 and d == D_MODEL and ctx.shape[1] == CTX_LEN
    l = 0
    lam_init = 0.8 - 0.6 * math.exp(-0.3 * l)

    cc = jnp.zeros((8, d), F32).at[:batch].set(c).at[batch].set(c_ctx)
    mod = _mod_call(cc, w_mod[l], b_mod[l][None, :])
    mod = mod.reshape(8, N_MOD, 1, d)
    sh1, sc1, ga1, sh2, sc2, ga2 = (mod[:batch, k] for k in range(N_MOD))
    csh1 = jnp.broadcast_to(mod[batch, 0][None], (batch, 1, d))
    csc1 = jnp.broadcast_to(mod[batch, 1][None], (batch, 1, d))

    wl = w_in[l]
    wp = jnp.concatenate([wl[:, :OFF_MG], wl[:, OFF_DQ:]], axis=1).astype(BF16)
    wgt = jnp.pad(wl[:, OFF_MG:OFF_DQ], ((0, 0), (0, LANES - N_GATES))).astype(BF16)

    cos_t, sin_t = _rope_tables(seq)
    g_pre = g_pre_mix[l][None, :]
    x2d = x.reshape(batch * seq, d)
    ctx2d = ctx.reshape(batch * CTX_LEN, d)

    p_lat, vt_lat, gt_lat = _inproj_call(x2d, g_pre, sh1, sc1, wp, wgt, cos_t, sin_t,
                                         tm=512, rows_per_batch=seq, rope=True, tk_v=ATT_TK)
    p_ctx, vt_ctx, gt_ctx = _inproj_call(ctx2d, g_pre, csh1, csc1, wp, wgt, cos_t, sin_t,
                                         tm=batch * CTX_LEN, rows_per_batch=CTX_LEN, rope=False, tk_v=CTX_LEN)

    m_lat = _mlstm_call(b_gate[l], p_lat, gt_lat, p_ctx, gt_ctx, conv_qk_w[l], conv_qk_b[l][None, :],
                        g_mlstm[l][None, :], batch=batch, seq=seq)

    lamp = jnp.zeros((8, LANES), F32)
    lamp = lamp.at[0, :DIFF_QK_DIM].set(lambda_q1[l]).at[1, :DIFF_QK_DIM].set(lambda_k1[l])
    lamp = lamp.at[2, :DIFF_QK_DIM].set(lambda_q2[l]).at[3, :DIFF_QK_DIM].set(lambda_k2[l])
    d_lat = _attn_call(p_lat, vt_lat, p_ctx, vt_ctx, lamp, g_diff[l][None, :],
                       batch=batch, seq=seq, lam_init=lam_init)

    wo = w_out[l].astype(BF16)
    x1, hff = _outproj_call(m_lat, d_lat, wo[:GROUP_WIDTH], wo[GROUP_WIDTH:], x2d, ga1, sh2, sc2,
                            g_post_mix[l][None, :], g_pre_ffn[l][None, :], tm=512, seq=seq)

    x2 = _ffn_call(hff, w_up[l].astype(BF16), conv_ffn_w[l], conv_ffn_b[l][None, :],
                   w_down[l].astype(BF16), x1, ga2, g_post_ffn[l][None, :], tm=512, tf=512, seq=seq)
    return x2.reshape(batch, seq, d)
```

```python
import functools
import math

import jax
import jax.numpy as jnp
import numpy as np
from jax import lax
from jax.experimental import pallas as pl
from jax.experimental.pallas import tpu as pltpu

F32 = jnp.float32
BF16 = jnp.bfloat16

D_MODEL = 2048
CTX_LEN = 256
GRID_W = 64
HEADS = 8
HEAD_DIM = 128
GROUP_WIDTH = HEADS * HEAD_DIM
CHUNK = 128
DIFF_QK_DIM = 64
ROPE_FREQS = 16
ROPE_BASE = 10000.0
D_FF = 5632
N_MOD = 6
EPS = 1e-6
N_GATES = 4 * HEADS

OFF_MG = 4 * GROUP_WIDTH
OFF_DQ = OFF_MG + N_GATES

LANES = 128
BF16_SUBLANES = 16
MXU_COLS = 256
VMEM_LIMIT = 56 * 1024 * 1024

TILE_MQ, TILE_MK, TILE_MV, TILE_MO, TILE_DQ, TILE_DK, TILE_DV = range(7)
N_PROJ_TILES = 7
P_TILES = 6

ATT_TQ = 2048
ATT_TK = 512
ATT_STRIP = 256
FFN_ROWS = 64
SHIFT_HEADROOM_LOG2 = 60.0
QK_SCALE_LOG2E = DIFF_QK_DIM ** -0.5 * 1.4426950408889634


def _cparams(sem, flags=None):
    return pltpu.CompilerParams(dimension_semantics=sem, vmem_limit_bytes=VMEM_LIMIT, flags=flags)


def _dot(a, b):
    return jnp.dot(a, b, preferred_element_type=F32)


def _dot_nt(a, b):
    return lax.dot_general(a, b, (((1,), (1,)), ((), ())), preferred_element_type=F32)


def _rms(y, g):
    return y * lax.rsqrt(jnp.mean(y * y, axis=-1, keepdims=True) + EPS) * g


def _mod_kernel(c_ref, w_ref, b_ref, o_ref):
    cc = c_ref[...]
    s = cc * jax.nn.sigmoid(cc)
    o_ref[...] = _dot(s.astype(BF16), w_ref[...].astype(BF16)) + b_ref[...]


def _mod_call(cc, w_mod, b_mod):
    n = w_mod.shape[1]
    tn = 1024
    return pl.pallas_call(
        _mod_kernel,
        grid=(n // tn,),
        in_specs=[pl.BlockSpec((8, D_MODEL), lambda j: (0, 0)),
                  pl.BlockSpec((D_MODEL, tn), lambda j: (0, j)),
                  pl.BlockSpec((1, tn), lambda j: (0, j))],
        out_specs=pl.BlockSpec((8, tn), lambda j: (0, j)),
        out_shape=jax.ShapeDtypeStruct((8, n), F32),
        compiler_params=_cparams(("parallel",)),
        name="mod",
    )(cc, w_mod, b_mod)


def _rope(acc, cos, sin_signed, first_half):
    out = []
    for c in range(acc.shape[1] // LANES):
        a = acc[:, c * LANES:(c + 1) * LANES]
        rot = jnp.where(first_half, pltpu.roll(a, LANES - ROPE_FREQS, 1), pltpu.roll(a, ROPE_FREQS, 1))
        out.append(a * cos + rot * sin_signed)
    return jnp.concatenate(out, axis=1)


def _inproj_kernel(x_ref, g_ref, sh_ref, sc_ref, w_ref, wgt_ref, cos_ref, sin_ref,
                   p_ref, vt_ref, gt_ref, xn_ref, *, rope, tk_v):
    j = pl.program_id(1)
    tm = x_ref.shape[0]

    @pl.when(j == 0)
    def _():
        x = x_ref[...]
        y = _rms(x, g_ref[...]) * (1.0 + sc_ref[0]) + sh_ref[0]
        xn = y.astype(BF16)
        xn_ref[...] = xn
        gt = _dot_nt(wgt_ref[...], xn)
        for c in range(tm // CHUNK):
            gt_ref[c] = gt[:, c * CHUNK:(c + 1) * CHUNK]

    acc = _dot_nt(xn_ref[...], w_ref[...])

    is_rope = (j == TILE_DQ) | (j == TILE_DK)
    is_v = j == TILE_DV

    if rope:
        @pl.when(is_rope)
        def _():
            lane = lax.broadcasted_iota(jnp.int32, (tm, LANES), 1)
            first_half = (lane % (2 * ROPE_FREQS)) < ROPE_FREQS
            fac = jnp.where(j == TILE_DQ, QK_SCALE_LOG2E, 1.0)
            p_ref[...] = (_rope(acc, cos_ref[...], sin_ref[...], first_half) * fac).astype(BF16)

        plain = jnp.logical_not(is_rope | is_v)
    else:
        plain = jnp.logical_not(is_v)

    @pl.when(plain)
    def _():
        p_ref[...] = acc.astype(BF16)

    @pl.when(is_v)
    def _():
        at = acc.T.astype(BF16)
        for c in range(tm // tk_v):
            vt_ref[c] = at[:, c * tk_v:(c + 1) * tk_v]


def _inproj_call(x2d, g, sh, sc, wp, wgt, cos_t, sin_t, *, tm, rows_per_batch, rope, tk_v):
    m = x2d.shape[0]
    blocks_per_batch = max(rows_per_batch // tm, 1)
    kern = functools.partial(_inproj_kernel, rope=rope, tk_v=tk_v)
    return pl.pallas_call(
        kern,
        grid=(m // tm, N_PROJ_TILES),
        in_specs=[
            pl.BlockSpec((tm, D_MODEL), lambda i, j: (i, 0)),
            pl.BlockSpec((1, D_MODEL), lambda i, j: (0, 0)),
            pl.BlockSpec((1, 1, D_MODEL), lambda i, j: (i // blocks_per_batch, 0, 0)),
            pl.BlockSpec((1, 1, D_MODEL), lambda i, j: (i // blocks_per_batch, 0, 0)),
            pl.BlockSpec((GROUP_WIDTH, D_MODEL), lambda i, j: (j, 0)),
            pl.BlockSpec((N_GATES, D_MODEL), lambda i, j: (0, 0)),
            pl.BlockSpec((tm, LANES), lambda i, j: (i % blocks_per_batch, 0)),
            pl.BlockSpec((tm, LANES), lambda i, j: (i % blocks_per_batch, 0)),
        ],
        out_specs=[
            pl.BlockSpec((tm, GROUP_WIDTH), lambda i, j: (i, jnp.minimum(j, P_TILES - 1))),
            pl.BlockSpec((tm // tk_v, GROUP_WIDTH, tk_v), lambda i, j: (i, 0, 0)),
            pl.BlockSpec((tm // CHUNK, N_GATES, CHUNK), lambda i, j: (i, 0, 0)),
        ],
        out_shape=[
            jax.ShapeDtypeStruct((m, P_TILES * GROUP_WIDTH), BF16),
            jax.ShapeDtypeStruct((m // tk_v, GROUP_WIDTH, tk_v), BF16),
            jax.ShapeDtypeStruct((m // CHUNK, N_GATES, CHUNK), F32),
        ],
        scratch_shapes=[pltpu.VMEM((tm, D_MODEL), BF16)],
        compiler_params=_cparams(("parallel", "arbitrary")),
        name="inproj_rope" if rope else "inproj_ctx",
    )(x2d, g, sh, sc, wp, wgt, cos_t, sin_t)


def _log_sigmoid(x):
    return jnp.minimum(x, 0.0) - jnp.log1p(jnp.exp(-jnp.abs(x)))


def _split_dot(row, tri):
    hi = row.astype(BF16)
    lo = (row - hi.astype(F32)).astype(BF16)
    hi8 = jnp.broadcast_to(hi, (BF16_SUBLANES, CHUNK))
    lo8 = jnp.broadcast_to(lo, (BF16_SUBLANES, CHUNK))
    return (_dot(hi8, tri) + _dot(lo8, tri))[0:1, :]


def _mlstm_kernel(bg_ref,
                  q_ref, k_ref, v_ref, o_ref, gt_ref,
                  kc_ref, vc_ref, gtc_ref,
                  cwq_ref, cwk_ref, cbq_ref, cbk_ref, gn_ref,
                  out_ref,
                  qs_ref, kst_ref, gs_ref, kstc_ref, gsc_ref, hf_ref, hb_ref, cst_ref):
    h = pl.program_id(1)
    n_lat = q_ref.shape[0] // CHUNK
    n_ctx = kc_ref.shape[0] // CHUNK

    row_i = lax.broadcasted_iota(jnp.int32, (CHUNK, CHUNK), 0)
    col_i = lax.broadcasted_iota(jnp.int32, (CHUNK, CHUNK), 1)
    tril = col_i <= row_i
    triu = col_i >= row_i
    tril_b = tril.astype(BF16)
    triu_b = triu.astype(BF16)
    ones_col = (col_i == 0).astype(BF16)
    first_row = row_i == 0
    last_row = row_i == CHUNK - 1

    def conv_silu(ref, c, n_chunks, w_ref, b_ref):
        r0 = pl.multiple_of(c * CHUNK, CHUNK)
        main = ref[pl.ds(r0, CHUNK), :].astype(F32)
        pstart = pl.multiple_of(jnp.maximum(r0 - BF16_SUBLANES, 0), BF16_SUBLANES)
        nstart = pl.multiple_of(jnp.minimum(r0 + CHUNK, (n_chunks - 1) * CHUNK), BF16_SUBLANES)
        prev_blk = ref[pl.ds(pstart, BF16_SUBLANES), :].astype(F32)
        next_blk = ref[pl.ds(nstart, BF16_SUBLANES), :].astype(F32)
        prev_row = prev_blk[BF16_SUBLANES - 1:BF16_SUBLANES, :] * jnp.where(c > 0, 1.0, 0.0)
        next_row = next_blk[0:1, :] * jnp.where(c < n_chunks - 1, 1.0, 0.0)
        up = jnp.where(first_row, prev_row, pltpu.roll(main, 1, 0))
        dn = jnp.where(last_row, next_row, pltpu.roll(main, CHUNK - 1, 0))
        w = w_ref[...]
        u = up * w[0:1, :] + main * w[1:2, :] + dn * w[2:3, :] + b_ref[...]
        return u * jax.nn.sigmoid(u)

    def gate_rows(g_ref3, dst_ref, c):
        def row(t):
            return g_ref3[c, pl.ds(t * HEADS + h, 1), :] + bg_ref[t, h]
        i_f, f_f, i_b, f_b = row(0), row(1), row(2), row(3)
        lf_f = _log_sigmoid(f_f)
        lf_b = _log_sigmoid(f_b)
        dst_ref[c, 0:1, :] = i_f - _split_dot(lf_f, triu_b)
        dst_ref[c, 1:2, :] = lf_f
        dst_ref[c, 2:3, :] = i_b - _split_dot(lf_b, tril_b)
        dst_ref[c, 3:4, :] = lf_b

    k_scale = HEAD_DIM ** -0.5

    def prep_lat(c, carry):
        qs_ref[c] = conv_silu(q_ref, c, n_lat, cwq_ref, cbq_ref).astype(BF16)
        kk = conv_silu(k_ref, c, n_lat, cwk_ref, cbk_ref) * k_scale
        kst_ref[c] = kk.T.astype(BF16)
        gate_rows(gt_ref, gs_ref, c)
        return carry

    lax.fori_loop(0, n_lat, prep_lat, 0, unroll=2)

    def prep_ctx(c, carry):
        kk = conv_silu(kc_ref, c, n_ctx, cwk_ref, cbk_ref) * k_scale
        kstc_ref[c] = kk.T.astype(BF16)
        gate_rows(gtc_ref, gsc_ref, c)
        return carry

    lax.fori_loop(0, n_ctx, prep_ctx, 0)

    def vaug(vref, c):
        r0 = pl.multiple_of(c * CHUNK, CHUNK)
        return jnp.concatenate([vref[pl.ds(r0, CHUNK), :], ones_col], axis=1)

    def state_update(d, m_prev, a_row, lf_row, kt, va):
        m_last = jnp.maximum(m_prev, jnp.max(a_row, axis=1, keepdims=True))
        b_last = jnp.sum(lf_row, axis=1, keepdims=True)
        w_row = jnp.exp(a_row - m_last)
        decay = jnp.exp(m_prev - m_last)
        kw = (kt.astype(F32) * w_row).astype(BF16)
        cst_ref[d] = decay * cst_ref[d] + _dot(kw, va)
        return b_last + m_last

    def chunk_step(d, c, m_prev, mask):
        a_row = gs_ref[c, 2 * d:2 * d + 1, :]
        lf_row = gs_ref[c, 2 * d + 1:2 * d + 2, :]
        q = qs_ref[c]
        kt = kst_ref[c]
        va = vaug(v_ref, c)
        a_mat = jnp.where(mask, a_row, -jnp.inf)
        m_col = jnp.maximum(jnp.max(a_mat, axis=1, keepdims=True), m_prev)
        w_mat = jnp.exp(a_mat - m_col)
        w_inter = jnp.exp(m_prev - m_col)
        b_col = jnp.sum(jnp.where(mask, lf_row, 0.0), axis=1, keepdims=True)
        s = (_dot(q, kt) * w_mat).astype(BF16)
        nd = w_inter * _dot(q, cst_ref[d].astype(BF16)) + _dot(s, va)
        num = nd[:, :HEAD_DIM]
        den = nd[:, HEAD_DIM:HEAD_DIM + 1]
        hh = num / jnp.maximum(jnp.abs(den), jnp.exp(-b_col - m_col))
        m_new = state_update(d, m_prev, a_row, lf_row, kt, va)
        return hh, m_new

    cst_ref[...] = jnp.zeros(cst_ref.shape, F32)

    def ctx_fwd(c, m):
        return state_update(0, m, gsc_ref[c, 0:1, :], gsc_ref[c, 1:2, :], kstc_ref[c], vaug(vc_ref, c))

    def ctx_bwd(i, m):
        c = n_ctx - 1 - i
        return state_update(1, m, gsc_ref[c, 2:3, :], gsc_ref[c, 3:4, :], kstc_ref[c], vaug(vc_ref, c))

    m_f = lax.fori_loop(0, n_ctx, ctx_fwd, jnp.zeros((1, 1), F32))
    m_b = lax.fori_loop(0, n_ctx, ctx_bwd, jnp.zeros((1, 1), F32))

    def main_step(i, carry):
        m_f, m_b = carry
        hf, m_f = chunk_step(0, i, m_f, tril)
        hf_ref[i] = hf
        cb = n_lat - 1 - i
        hb, m_b = chunk_step(1, cb, m_b, triu)
        hb_ref[cb] = hb
        return m_f, m_b

    lax.fori_loop(0, n_lat, main_step, (m_f, m_b), unroll=8)

    def finish(c, carry):
        r0 = pl.multiple_of(c * CHUNK, CHUNK)
        hh = _rms(hf_ref[c] + hb_ref[c], gn_ref[...])
        og = o_ref[pl.ds(r0, CHUNK), :].astype(F32)
        out_ref[pl.ds(r0, CHUNK), :] = (hh * jax.nn.sigmoid(og)).astype(BF16)
        return carry

    lax.fori_loop(0, n_lat, finish, 0, unroll=4)


def _mlstm_call(bg, p_lat, gt_lat, p_ctx, gt_ctx, cw, cb, gn, *, batch, seq):
    n_lat = seq // CHUNK
    n_ctx = CTX_LEN // CHUNK

    def col(tile):
        return lambda b, h: (b, tile * HEADS + h)

    lat_blk = (seq, HEAD_DIM)
    ctx_blk = (CTX_LEN, HEAD_DIM)
    return pl.pallas_call(
        _mlstm_kernel,
        grid=(batch, HEADS),
        in_specs=[
            pl.BlockSpec(memory_space=pltpu.SMEM),
            pl.BlockSpec(lat_blk, col(TILE_MQ)),
            pl.BlockSpec(lat_blk, col(TILE_MK)),
            pl.BlockSpec(lat_blk, col(TILE_MV)),
            pl.BlockSpec(lat_blk, col(TILE_MO)),
            pl.BlockSpec((n_lat, N_GATES, CHUNK), lambda b, h: (b, 0, 0)),
            pl.BlockSpec(ctx_blk, col(TILE_MK)),
            pl.BlockSpec(ctx_blk, col(TILE_MV)),
            pl.BlockSpec((n_ctx, N_GATES, CHUNK), lambda b, h: (b, 0, 0)),
            pl.BlockSpec((3, HEAD_DIM), lambda b, h: (0, h)),
            pl.BlockSpec((3, HEAD_DIM), lambda b, h: (0, HEADS + h)),
            pl.BlockSpec((1, HEAD_DIM), lambda b, h: (0, h)),
            pl.BlockSpec((1, HEAD_DIM), lambda b, h: (0, HEADS + h)),
            pl.BlockSpec((1, HEAD_DIM), lambda b, h: (0, h)),
        ],
        out_specs=pl.BlockSpec(lat_blk, lambda b, h: (b, h)),
        out_shape=jax.ShapeDtypeStruct((batch * seq, GROUP_WIDTH), BF16),
        scratch_shapes=[
            pltpu.VMEM((n_lat, CHUNK, HEAD_DIM), BF16),
            pltpu.VMEM((n_lat, HEAD_DIM, CHUNK), BF16),
            pltpu.VMEM((n_lat, 8, CHUNK), F32),
            pltpu.VMEM((n_ctx, HEAD_DIM, CHUNK), BF16),
            pltpu.VMEM((n_ctx, 8, CHUNK), F32),
            pltpu.VMEM((n_lat, CHUNK, HEAD_DIM), F32),
            pltpu.VMEM((n_lat, CHUNK, HEAD_DIM), F32),
            pltpu.VMEM((2, HEAD_DIM, 2 * HEAD_DIM), F32),
        ],
        compiler_params=_cparams(("parallel", "parallel")),
        name="mlstm",
    )(bg, p_lat, p_lat, p_lat, p_lat, gt_lat, p_ctx, p_ctx, gt_ctx, cw, cw, cb, cb, gn)


def _attn_kernel(q_ref, k_ref, vt_ref, kc_ref, vtc_ref, lam_ref, gd_ref, o_ref,
                 qq_ref, m_ref, acc_ref, s0_ref, s1_ref, cm0_ref, cm1_ref,
                 p0_ref, p1_ref, al0_ref, al1_ref, l_ref, *, lam_init):
    tq = q_ref.shape[0]
    n_kv = k_ref.shape[0] // ATT_TK

    lane = lax.broadcasted_iota(jnp.int32, (tq, HEAD_DIM), 1)
    q = q_ref[...]
    zero = jnp.zeros_like(q)
    qq_ref[0:tq, :] = jnp.where(lane < DIFF_QK_DIM, q, zero)
    qq_ref[tq:2 * tq, :] = jnp.where(lane >= DIFF_QK_DIM, q, zero)

    sbufs = ((s0_ref, cm0_ref), (s1_ref, cm1_ref))
    pbufs = ((p0_ref, al0_ref), (p1_ref, al1_ref))
    strips = [slice(c * ATT_STRIP, (c + 1) * ATT_STRIP) for c in range(2 * tq // ATT_STRIP)]

    def k_lat(t):
        return k_ref[pl.ds(pl.multiple_of(t * ATT_TK, ATT_TK), ATT_TK), :]

    def with_ones(vt):
        return jnp.concatenate([vt, jnp.ones((BF16_SUBLANES, vt.shape[1]), BF16)], axis=0)

    k_abs = jnp.maximum(jnp.max(jnp.abs(k_ref[...]), axis=0, keepdims=True),
                        jnp.max(jnp.abs(kc_ref[...]), axis=0, keepdims=True))
    bound = _dot_nt(jnp.broadcast_to(k_abs, (BF16_SUBLANES, HEAD_DIM)), jnp.abs(qq_ref[...]))
    m_ref[...] = bound[0:1, :] - SHIFT_HEADROOM_LOG2
    acc_ref[...] = jnp.zeros(acc_ref.shape, F32)

    l_ref[...] = jnp.zeros(l_ref.shape, F32)

    def shifted_weights(k, buf, cs):
        st = _dot_nt(k, qq_ref[cs, :])
        w = jnp.exp2(st - m_ref[:, cs])
        l_ref[:, cs] += jnp.sum(w, axis=0, keepdims=True)
        pbufs[buf][0][0:k.shape[0], cs] = w.astype(BF16)

    def accumulate(vt, buf, cs):
        acc_ref[0:HEAD_DIM, cs] += _dot(vt, pbufs[buf][0][0:vt.shape[1], cs])

    def fixed_substep(k, w_buf, vt, pv_buf):
        for cs in strips:
            if k is not None:
                shifted_weights(k, w_buf, cs)
            if vt is not None:
                accumulate(vt, pv_buf, cs)

    fixed_substep(kc_ref[...], 0, None, None)
    fixed_substep(k_lat(0), 1, vtc_ref[0], 0)

    def fixed_body(i, carry):
        t = 2 * i
        fixed_substep(k_lat(t + 1), 0, vt_ref[t], 1)
        fixed_substep(k_lat(t + 2), 1, vt_ref[t + 1], 0)
        return carry

    lax.fori_loop(0, n_kv // 2 - 1, fixed_body, 0)
    fixed_substep(k_lat(n_kv - 1), 0, vt_ref[n_kv - 2], 1)
    fixed_substep(None, None, vt_ref[n_kv - 1], 0)

    l_min = jnp.min(l_ref[...])
    in_range = l_min >= 2.0 ** -SHIFT_HEADROOM_LOG2

    def score(k, buf, cs):
        s_ref, cm_ref = sbufs[buf]
        st = _dot_nt(k, qq_ref[cs, :])
        s_ref[0:k.shape[0], cs] = st
        cm_ref[:, cs] = jnp.max(st, axis=0, keepdims=True)

    def softmax(buf, rows, cs):
        s_ref, cm_ref = sbufs[buf]
        p_ref, al_ref = pbufs[buf]
        m_prev = m_ref[:, cs]
        m_new = jnp.maximum(m_prev, cm_ref[:, cs])
        p_ref[0:rows, cs] = jnp.exp2((s_ref[0:rows, cs] - m_new).astype(BF16))
        al_ref[:, cs] = jnp.exp2(m_prev - m_new)
        m_ref[:, cs] = m_new

    def pv(vt_aug, buf, cs):
        p_ref, al_ref = pbufs[buf]
        rows = vt_aug.shape[1]
        acc_ref[:, cs] = al_ref[:, cs] * acc_ref[:, cs] + _dot(vt_aug, p_ref[0:rows, cs])

    def substep(k, score_buf, soft_buf, soft_rows, vt, pv_buf):
        if vt is not None:
            vt_aug = with_ones(vt)
        for cs in strips:
            if k is not None:
                score(k, score_buf, cs)
            if soft_buf is not None:
                softmax(soft_buf, soft_rows, cs)
            if vt is not None:
                pv(vt_aug, pv_buf, cs)

    def body(i, carry):
        t = 2 * i
        substep(k_lat(t + 2), 1, 0, ATT_TK, vt_ref[t], 1)
        substep(k_lat(t + 3), 0, 1, ATT_TK, vt_ref[t + 1], 0)
        return carry

    @pl.when(jnp.logical_not(in_range))
    def _():
        m_ref[...] = jnp.full(m_ref.shape, -jnp.inf, F32)
        acc_ref[...] = jnp.zeros(acc_ref.shape, F32)
        substep(kc_ref[...], 0, None, None, None, None)
        substep(k_lat(0), 1, 0, CTX_LEN, None, None)
        substep(k_lat(1), 0, 1, ATT_TK, vtc_ref[0], 0)
        lax.fori_loop(0, n_kv // 2 - 1, body, 0)
        substep(None, None, 0, ATT_TK, vt_ref[n_kv - 2], 1)
        substep(None, None, None, None, vt_ref[n_kv - 1], 0)
        l_ref[...] = acc_ref[HEAD_DIM:HEAD_DIM + 1, :]

    lp = lam_ref[...]
    e1 = jnp.exp(jnp.sum(lp[0:1, :] * lp[1:2, :], axis=1, keepdims=True))
    e2 = jnp.exp(jnp.sum(lp[2:3, :] * lp[3:4, :], axis=1, keepdims=True))
    lam = e1 - e2 + lam_init
    o = acc_ref[0:HEAD_DIM, :] / l_ref[...]
    od = o[:, 0:tq] - lam * o[:, tq:2 * tq]
    od = od * lax.rsqrt(jnp.mean(od * od, axis=0, keepdims=True) + EPS)
    o_ref[...] = (od.T * gd_ref[...] * (1.0 - lam_init)).astype(BF16)


def _attn_call(p_lat, vt_lat, p_ctx, vt_ctx, lamp, gd, *, batch, seq, lam_init):
    nq = seq // ATT_TQ
    n_kv = seq // ATT_TK
    kern = functools.partial(_attn_kernel, lam_init=lam_init)
    return pl.pallas_call(
        kern,
        grid=(batch, HEADS, nq),
        in_specs=[
            pl.BlockSpec((ATT_TQ, HEAD_DIM), lambda b, h, i: (b * nq + i, TILE_DQ * HEADS + h)),
            pl.BlockSpec((seq, HEAD_DIM), lambda b, h, i: (b, TILE_DK * HEADS + h)),
            pl.BlockSpec((n_kv, HEAD_DIM, ATT_TK), lambda b, h, i: (b, h, 0)),
            pl.BlockSpec((CTX_LEN, HEAD_DIM), lambda b, h, i: (b, TILE_DK * HEADS + h)),
            pl.BlockSpec((1, HEAD_DIM, CTX_LEN), lambda b, h, i: (b, h, 0)),
            pl.BlockSpec((8, LANES), lambda b, h, i: (0, 0)),
            pl.BlockSpec((1, HEAD_DIM), lambda b, h, i: (0, 0)),
        ],
        out_specs=pl.BlockSpec((ATT_TQ, HEAD_DIM), lambda b, h, i: (b * nq + i, h)),
        out_shape=jax.ShapeDtypeStruct((batch * seq, GROUP_WIDTH), BF16),
        scratch_shapes=[
            pltpu.VMEM((2 * ATT_TQ, HEAD_DIM), BF16),
            pltpu.VMEM((1, 2 * ATT_TQ), F32),
            pltpu.VMEM((HEAD_DIM + BF16_SUBLANES, 2 * ATT_TQ), F32),
            pltpu.VMEM((ATT_TK, 2 * ATT_TQ), F32),
            pltpu.VMEM((ATT_TK, 2 * ATT_TQ), F32),
            pltpu.VMEM((1, 2 * ATT_TQ), F32),
            pltpu.VMEM((1, 2 * ATT_TQ), F32),
            pltpu.VMEM((ATT_TK, 2 * ATT_TQ), BF16),
            pltpu.VMEM((ATT_TK, 2 * ATT_TQ), BF16),
            pltpu.VMEM((1, 2 * ATT_TQ), F32),
            pltpu.VMEM((1, 2 * ATT_TQ), F32),
            pltpu.VMEM((1, 2 * ATT_TQ), F32),
        ],
        compiler_params=_cparams(("parallel", "parallel", "arbitrary")),
        name="diffattn",
    )(p_lat, p_lat, vt_lat, p_ctx, vt_ctx, lamp, gd)


def _outproj_kernel(m_ref, d_ref, wa_ref, wb_ref, x_ref, ga_ref, sh_ref, sc_ref, gpost_ref, gpre_ref,
                    x1_ref, h_ref):
    y = _dot(m_ref[...], wa_ref[...]) + _dot(d_ref[...], wb_ref[...])
    x1 = x_ref[...] + ga_ref[0] * _rms(y, gpost_ref[...])
    x1_ref[...] = x1
    hh = _rms(x1, gpre_ref[...]) * (1.0 + sc_ref[0]) + sh_ref[0]
    h_ref[...] = hh.astype(BF16)


def _outproj_call(m_lat, d_lat, wa, wb, x2d, ga1, sh2, sc2, gpost, gpre, *, tm, seq):
    m = x2d.shape[0]
    bpb = seq // tm
    row = lambda i: (i, 0)
    fixed = lambda i: (0, 0)
    per_batch = lambda i: (i // bpb, 0, 0)
    return pl.pallas_call(
        _outproj_kernel,
        grid=(m // tm,),
        in_specs=[
            pl.BlockSpec((tm, GROUP_WIDTH), row),
            pl.BlockSpec((tm, GROUP_WIDTH), row),
            pl.BlockSpec((GROUP_WIDTH, D_MODEL), fixed),
            pl.BlockSpec((GROUP_WIDTH, D_MODEL), fixed),
            pl.BlockSpec((tm, D_MODEL), row),
            pl.BlockSpec((1, 1, D_MODEL), per_batch),
            pl.BlockSpec((1, 1, D_MODEL), per_batch),
            pl.BlockSpec((1, 1, D_MODEL), per_batch),
            pl.BlockSpec((1, D_MODEL), fixed),
            pl.BlockSpec((1, D_MODEL), fixed),
        ],
        out_specs=[pl.BlockSpec((tm, D_MODEL), row), pl.BlockSpec((tm, D_MODEL), row)],
        out_shape=[jax.ShapeDtypeStruct((m, D_MODEL), F32), jax.ShapeDtypeStruct((m, D_MODEL), BF16)],
        compiler_params=_cparams(("parallel",)),
        name="outproj",
    )(m_lat, d_lat, wa, wb, x2d, ga1, sh2, sc2, gpost, gpre)


def _ffn_kernel(h_ref, hp_ref, hn_ref, wg_ref, wv_ref, cwg_ref, cwv_ref, cbg_ref, cbv_ref, wd_ref,
                x1_ref, ga_ref, gpost_ref, o_ref, lhs_ref, acc_ref, ug_ref, uv_ref, a_ref, *, blocks_per_seq):
    i = pl.program_id(0)
    f = pl.program_id(1)
    tm = h_ref.shape[0]
    halo = BF16_SUBLANES

    @pl.when(f == 0)
    def _():
        lhs_ref[0:halo, :] = hp_ref[...]
        lhs_ref[halo:halo + tm, :] = h_ref[...]
        lhs_ref[halo + tm:2 * halo + tm, :] = hn_ref[...]
        acc_ref[...] = jnp.zeros(acc_ref.shape, F32)

    has_prev = ((i % blocks_per_seq) > 0).astype(F32)
    has_next = ((i % blocks_per_seq) < blocks_per_seq - 1).astype(F32)

    def up_project(w_ref, u_ref):
        u_ref[...] = _dot(lhs_ref[...], w_ref[...])
        u_ref[halo - 1:halo, :] = u_ref[halo - 1:halo, :] * has_prev
        u_ref[halo + tm:halo + tm + 1, :] = u_ref[halo + tm:halo + tm + 1, :] * has_next

    up_project(wg_ref, ug_ref)
    up_project(wv_ref, uv_ref)

    def conv(u_ref, r, cw, cb):
        lo = halo + r
        return (u_ref[lo - 1:lo - 1 + FFN_ROWS, :] * cw[0:1, :] + u_ref[lo:lo + FFN_ROWS, :] * cw[1:2, :]
                + u_ref[lo + 1:lo + 1 + FFN_ROWS, :] * cw[2:3, :] + cb)

    cwg, cwv, cbg, cbv = cwg_ref[...], cwv_ref[...], cbg_ref[...], cbv_ref[...]
    half = tm // 2
    for r0 in (0, half):
        for r in range(r0, r0 + half, FFN_ROWS):
            gate = conv(ug_ref, r, cwg, cbg)
            val = conv(uv_ref, r, cwv, cbv)
            a_ref[r:r + FFN_ROWS, :] = (gate * jax.nn.sigmoid(gate) * val).astype(BF16)
        acc_ref[r0:r0 + half, :] += _dot(a_ref[r0:r0 + half, :], wd_ref[...])

    @pl.when(f == pl.num_programs(1) - 1)
    def _():
        o_ref[...] = x1_ref[...] + ga_ref[0] * _rms(acc_ref[...], gpost_ref[...])


def _ffn_call(hff, w_up, cw, cb, w_down, x1, ga2, gpost, *, tm, tf, seq):
    m = hff.shape[0]
    nf = D_FF // tf
    bps = seq // tm
    halo = BF16_SUBLANES
    hb = tm // halo
    n_halo = m // halo
    kern = functools.partial(_ffn_kernel, blocks_per_seq=bps)
    row = lambda i, f: (i, 0)
    fixed = lambda i, f: (0, 0)
    return pl.pallas_call(
        kern,
        grid=(m // tm, nf),
        in_specs=[
            pl.BlockSpec((tm, D_MODEL), row),
            pl.BlockSpec((halo, D_MODEL), lambda i, f: (jnp.maximum(i * hb - 1, 0), 0)),
            pl.BlockSpec((halo, D_MODEL), lambda i, f: (jnp.minimum((i + 1) * hb, n_halo - 1), 0)),
            pl.BlockSpec((D_MODEL, tf), lambda i, f: (0, f)),
            pl.BlockSpec((D_MODEL, tf), lambda i, f: (0, nf + f)),
            pl.BlockSpec((3, tf), lambda i, f: (0, f)),
            pl.BlockSpec((3, tf), lambda i, f: (0, nf + f)),
            pl.BlockSpec((1, tf), lambda i, f: (0, f)),
            pl.BlockSpec((1, tf), lambda i, f: (0, nf + f)),
            pl.BlockSpec((tf, D_MODEL), lambda i, f: (f, 0)),
            pl.BlockSpec((tm, D_MODEL), row),
            pl.BlockSpec((1, 1, D_MODEL), lambda i, f: (i // bps, 0, 0)),
            pl.BlockSpec((1, D_MODEL), fixed),
        ],
        out_specs=pl.BlockSpec((tm, D_MODEL), row),
        out_shape=jax.ShapeDtypeStruct((m, D_MODEL), F32),
        scratch_shapes=[pltpu.VMEM((tm + 2 * halo, D_MODEL), BF16), pltpu.VMEM((tm, D_MODEL), F32),
                        pltpu.VMEM((tm + 2 * halo, tf), F32), pltpu.VMEM((tm + 2 * halo, tf), F32),
                        pltpu.VMEM((tm, tf), BF16)],
        compiler_params=_cparams(("parallel", "arbitrary")),
        name="convffn",
    )(hff, hff, hff, w_up, w_up, cw, cw, cb, cb, w_down, x1, ga2, gpost)


def _rope_tables(seq):
    rows = seq // GRID_W
    r = np.repeat(np.arange(rows, dtype=np.float64), GRID_W)
    c = np.tile(np.arange(GRID_W, dtype=np.float64), rows)
    inv_freq = ROPE_BASE ** (-np.arange(ROPE_FREQS, dtype=np.float64) / ROPE_FREQS)
    ar = r[:, None] * inv_freq
    ac = c[:, None] * inv_freq
    ang = np.concatenate([ar, ar, ac, ac, ar, ar, ac, ac], axis=1)
    sign = np.tile(np.concatenate([-np.ones(ROPE_FREQS), np.ones(ROPE_FREQS)]), 4)
    return jnp.asarray(np.cos(ang), F32), jnp.asarray(np.sin(ang) * sign, F32)


def kernel(x, c, ctx, c_ctx, w_mod, b_mod, g_pre_mix, g_post_mix, w_in, b_gate, conv_qk_w, conv_qk_b,
           g_mlstm, lambda_q1, lambda_k1, lambda_q2, lambda_k2, g_diff, w_out, g_pre_ffn, g_post_ffn,
           w_up, conv_ffn_w, conv_ffn_b, w_down):
    batch, seq, d = x.shape
    depth = w_mod.shape[0]
    assert depth == 1 and d == D_MODEL and ctx.shape[1] == CTX_LEN
    l = 0
    lam_init = 0.8 - 0.6 * math.exp(-0.3 * l)

    cc = jnp.zeros((8, d), F32).at[:batch].set(c).at[batch].set(c_ctx)
    mod = _mod_call(cc, w_mod[l], b_mod[l][None, :])
    mod = mod.reshape(8, N_MOD, 1, d)
    sh1, sc1, ga1, sh2, sc2, ga2 = (mod[:batch, k] for k in range(N_MOD))
    csh1 = jnp.broadcast_to(mod[batch, 0][None], (batch, 1, d))
    csc1 = jnp.broadcast_to(mod[batch, 1][None], (batch, 1, d))

    wt = w_in[l].T
    wp = jnp.concatenate([wt[:OFF_MG], wt[OFF_DQ:]], axis=0).astype(BF16)
    wgt = wt[OFF_MG:OFF_DQ].astype(BF16)

    cos_t, sin_t = _rope_tables(seq)
    g_pre = g_pre_mix[l][None, :]
    x2d = x.reshape(batch * seq, d)
    ctx2d = ctx.reshape(batch * CTX_LEN, d)

    p_lat, vt_lat, gt_lat = _inproj_call(x2d, g_pre, sh1, sc1, wp, wgt, cos_t, sin_t,
                                         tm=512, rows_per_batch=seq, rope=True, tk_v=ATT_TK)
    p_ctx, vt_ctx, gt_ctx = _inproj_call(ctx2d, g_pre, csh1, csc1, wp, wgt, cos_t, sin_t,
                                         tm=batch * CTX_LEN, rows_per_batch=CTX_LEN, rope=False, tk_v=CTX_LEN)

    m_lat = _mlstm_call(b_gate[l], p_lat, gt_lat, p_ctx, gt_ctx, conv_qk_w[l], conv_qk_b[l][None, :],
                        g_mlstm[l][None, :], batch=batch, seq=seq)

    lamp = jnp.zeros((8, LANES), F32)
    lamp = lamp.at[0, :DIFF_QK_DIM].set(lambda_q1[l]).at[1, :DIFF_QK_DIM].set(lambda_k1[l])
    lamp = lamp.at[2, :DIFF_QK_DIM].set(lambda_q2[l]).at[3, :DIFF_QK_DIM].set(lambda_k2[l])
    d_lat = _attn_call(p_lat, vt_lat, p_ctx, vt_ctx, lamp, g_diff[l][None, :],
                       batch=batch, seq=seq, lam_init=lam_init)

    wo = w_out[l].astype(BF16)
    x1, hff = _outproj_call(m_lat, d_lat, wo[:GROUP_WIDTH], wo[GROUP_WIDTH:], x2d, ga1, sh2, sc2,
                            g_post_mix[l][None, :], g_pre_ffn[l][None, :], tm=512, seq=seq)

    x2 = _ffn_call(hff, w_up[l].astype(BF16), conv_ffn_w[l], conv_ffn_b[l][None, :],
                   w_down[l].astype(BF16), x1, ga2, g_post_ffn[l][None, :], tm=512, tf=512, seq=seq)
    return x2.reshape(batch, seq, d)
```

```python
import functools
import math

import jax
import jax.numpy as jnp
import numpy as np
from jax import lax
from jax.experimental import pallas as pl
from jax.experimental.pallas import tpu as pltpu

F32 = jnp.float32
BF16 = jnp.bfloat16

D_MODEL = 2048
CTX_LEN = 256
GRID_W = 64
HEADS = 8
HEAD_DIM = 128
GROUP_WIDTH = HEADS * HEAD_DIM
CHUNK = 128
DIFF_QK_DIM = 64
ROPE_FREQS = 16
ROPE_BASE = 10000.0
D_FF = 5632
N_MOD = 6
EPS = 1e-6
N_GATES = 4 * HEADS

OFF_MG = 4 * GROUP_WIDTH
OFF_DQ = OFF_MG + N_GATES

LANES = 128
BF16_SUBLANES = 16
VMEM_LIMIT = 56 * 1024 * 1024

TILE_MQ, TILE_MK, TILE_MV, TILE_MO, TILE_DQ, TILE_DK, TILE_DV = range(7)
N_PROJ_TILES = 7
P_TILES = 6

ATT_TQ = 2048
ATT_TK = 512
ATT_STRIP = 256
FFN_ROWS = 64
SHIFT_HEADROOM_LOG2 = 60.0
QK_SCALE_LOG2E = DIFF_QK_DIM ** -0.5 * 1.4426950408889634


def _cparams(sem):
    return pltpu.CompilerParams(dimension_semantics=sem, vmem_limit_bytes=VMEM_LIMIT)


def _dot(a, b):
    return jnp.dot(a, b, preferred_element_type=F32)


def _dot_nt(a, b):
    return lax.dot_general(a, b, (((1,), (1,)), ((), ())), preferred_element_type=F32)


def _rms(y, g):
    return y * lax.rsqrt(jnp.mean(y * y, axis=-1, keepdims=True) + EPS) * g


def _mod_kernel(c_ref, w_ref, b_ref, o_ref):
    cc = c_ref[...]
    s = cc * jax.nn.sigmoid(cc)
    o_ref[...] = _dot(s.astype(BF16), w_ref[...].astype(BF16)) + b_ref[...]


def _mod_call(cc, w_mod, b_mod):
    n = w_mod.shape[1]
    tn = 1024
    return pl.pallas_call(
        _mod_kernel,
        grid=(n // tn,),
        in_specs=[pl.BlockSpec((8, D_MODEL), lambda j: (0, 0)),
                  pl.BlockSpec((D_MODEL, tn), lambda j: (0, j)),
                  pl.BlockSpec((1, tn), lambda j: (0, j))],
        out_specs=pl.BlockSpec((8, tn), lambda j: (0, j)),
        out_shape=jax.ShapeDtypeStruct((8, n), F32),
        compiler_params=_cparams(("parallel",)),
        name="mod",
    )(cc, w_mod, b_mod)


def _rope(acc, cos, sin_signed, first_half):
    out = []
    for c in range(acc.shape[1] // LANES):
        a = acc[:, c * LANES:(c + 1) * LANES]
        rot = jnp.where(first_half, pltpu.roll(a, LANES - ROPE_FREQS, 1), pltpu.roll(a, ROPE_FREQS, 1))
        out.append(a * cos + rot * sin_signed)
    return jnp.concatenate(out, axis=1)


def _inproj_kernel(x_ref, g_ref, sh_ref, sc_ref, w_ref, wgt_ref, cos_ref, sin_ref,
                   p_ref, vt_ref, gt_ref, xn_ref, *, rope, tk_v):
    j = pl.program_id(1)
    tm = x_ref.shape[0]

    @pl.when(j == 0)
    def _():
        x = x_ref[...]
        y = _rms(x, g_ref[...]) * (1.0 + sc_ref[0]) + sh_ref[0]
        xn = y.astype(BF16)
        xn_ref[...] = xn
        gt = _dot_nt(wgt_ref[...], xn)
        for c in range(tm // CHUNK):
            gt_ref[c] = gt[:, c * CHUNK:(c + 1) * CHUNK]

    acc = _dot_nt(xn_ref[...], w_ref[...])

    is_rope = (j == TILE_DQ) | (j == TILE_DK)
    is_v = j == TILE_DV

    if rope:
        @pl.when(is_rope)
        def _():
            lane = lax.broadcasted_iota(jnp.int32, (tm, LANES), 1)
            first_half = (lane % (2 * ROPE_FREQS)) < ROPE_FREQS
            fac = jnp.where(j == TILE_DQ, QK_SCALE_LOG2E, 1.0)
            p_ref[...] = (_rope(acc, cos_ref[...], sin_ref[...], first_half) * fac).astype(BF16)

        plain = jnp.logical_not(is_rope | is_v)
    else:
        plain = jnp.logical_not(is_v)

    @pl.when(plain)
    def _():
        p_ref[...] = acc.astype(BF16)

    @pl.when(is_v)
    def _():
        at = acc.T.astype(BF16)
        for c in range(tm // tk_v):
            vt_ref[c] = at[:, c * tk_v:(c + 1) * tk_v]


def _inproj_call(x2d, g, sh, sc, wp, wgt, cos_t, sin_t, *, tm, rows_per_batch, rope, tk_v):
    m = x2d.shape[0]
    blocks_per_batch = max(rows_per_batch // tm, 1)
    kern = functools.partial(_inproj_kernel, rope=rope, tk_v=tk_v)
    return pl.pallas_call(
        kern,
        grid=(m // tm, N_PROJ_TILES),
        in_specs=[
            pl.BlockSpec((tm, D_MODEL), lambda i, j: (i, 0)),
            pl.BlockSpec((1, D_MODEL), lambda i, j: (0, 0)),
            pl.BlockSpec((1, 1, D_MODEL), lambda i, j: (i // blocks_per_batch, 0, 0)),
            pl.BlockSpec((1, 1, D_MODEL), lambda i, j: (i // blocks_per_batch, 0, 0)),
            pl.BlockSpec((GROUP_WIDTH, D_MODEL), lambda i, j: (j, 0)),
            pl.BlockSpec((N_GATES, D_MODEL), lambda i, j: (0, 0)),
            pl.BlockSpec((tm, LANES), lambda i, j: (i % blocks_per_batch, 0)),
            pl.BlockSpec((tm, LANES), lambda i, j: (i % blocks_per_batch, 0)),
        ],
        out_specs=[
            pl.BlockSpec((tm, GROUP_WIDTH), lambda i, j: (i, jnp.minimum(j, P_TILES - 1))),
            pl.BlockSpec((tm // tk_v, GROUP_WIDTH, tk_v), lambda i, j: (i, 0, 0)),
            pl.BlockSpec((tm // CHUNK, N_GATES, CHUNK), lambda i, j: (i, 0, 0)),
        ],
        out_shape=[
            jax.ShapeDtypeStruct((m, P_TILES * GROUP_WIDTH), BF16),
            jax.ShapeDtypeStruct((m // tk_v, GROUP_WIDTH, tk_v), BF16),
            jax.ShapeDtypeStruct((m // CHUNK, N_GATES, CHUNK), F32),
        ],
        scratch_shapes=[pltpu.VMEM((tm, D_MODEL), BF16)],
        compiler_params=_cparams(("parallel", "arbitrary")),
        name="inproj_rope" if rope else "inproj_ctx",
    )(x2d, g, sh, sc, wp, wgt, cos_t, sin_t)


def _log_sigmoid(x):
    return jnp.minimum(x, 0.0) - jnp.log1p(jnp.exp(-jnp.abs(x)))


def _split_dot(row, tri):
    hi = row.astype(BF16)
    lo = (row - hi.astype(F32)).astype(BF16)
    hi8 = jnp.broadcast_to(hi, (BF16_SUBLANES, CHUNK))
    lo8 = jnp.broadcast_to(lo, (BF16_SUBLANES, CHUNK))
    return (_dot(hi8, tri) + _dot(lo8, tri))[0:1, :]


def _mlstm_kernel(bg_ref,
                  q_ref, k_ref, v_ref, o_ref, gt_ref,
                  kc_ref, vc_ref, gtc_ref,
                  cwq_ref, cwk_ref, cbq_ref, cbk_ref, gn_ref,
                  out_ref,
                  qs_ref, kst_ref, gs_ref, kstc_ref, gsc_ref, hf_ref, hb_ref, cst_ref):
    h = pl.program_id(1)
    n_lat = q_ref.shape[0] // CHUNK
    n_ctx = kc_ref.shape[0] // CHUNK

    row_i = lax.broadcasted_iota(jnp.int32, (CHUNK, CHUNK), 0)
    col_i = lax.broadcasted_iota(jnp.int32, (CHUNK, CHUNK), 1)
    tril = col_i <= row_i
    triu = col_i >= row_i
    tril_b = tril.astype(BF16)
    triu_b = triu.astype(BF16)
    ones_col = (col_i == 0).astype(BF16)
    first_row = row_i == 0
    last_row = row_i == CHUNK - 1

    def conv_silu(ref, c, n_chunks, w_ref, b_ref):
        r0 = pl.multiple_of(c * CHUNK, CHUNK)
        main = ref[pl.ds(r0, CHUNK), :].astype(F32)
        pstart = pl.multiple_of(jnp.maximum(r0 - BF16_SUBLANES, 0), BF16_SUBLANES)
        nstart = pl.multiple_of(jnp.minimum(r0 + CHUNK, (n_chunks - 1) * CHUNK), BF16_SUBLANES)
        prev_blk = ref[pl.ds(pstart, BF16_SUBLANES), :].astype(F32)
        next_blk = ref[pl.ds(nstart, BF16_SUBLANES), :].astype(F32)
        prev_row = prev_blk[BF16_SUBLANES - 1:BF16_SUBLANES, :] * jnp.where(c > 0, 1.0, 0.0)
        next_row = next_blk[0:1, :] * jnp.where(c < n_chunks - 1, 1.0, 0.0)
        up = jnp.where(first_row, prev_row, pltpu.roll(main, 1, 0))
        dn = jnp.where(last_row, next_row, pltpu.roll(main, CHUNK - 1, 0))
        w = w_ref[...]
        u = up * w[0:1, :] + main * w[1:2, :] + dn * w[2:3, :] + b_ref[...]
        return u * jax.nn.sigmoid(u)

    def gate_rows(g_ref3, dst_ref, c):
        def row(t):
            return g_ref3[c, pl.ds(t * HEADS + h, 1), :] + bg_ref[t, h]
        i_f, f_f, i_b, f_b = row(0), row(1), row(2), row(3)
        lf_f = _log_sigmoid(f_f)
        lf_b = _log_sigmoid(f_b)
        dst_ref[c, 0:1, :] = i_f - _split_dot(lf_f, triu_b)
        dst_ref[c, 1:2, :] = lf_f
        dst_ref[c, 2:3, :] = i_b - _split_dot(lf_b, tril_b)
        dst_ref[c, 3:4, :] = lf_b

    k_scale = HEAD_DIM ** -0.5

    def prep_lat(c, carry):
        qs_ref[c] = conv_silu(q_ref, c, n_lat, cwq_ref, cbq_ref).astype(BF16)
        kk = conv_silu(k_ref, c, n_lat, cwk_ref, cbk_ref) * k_scale
        kst_ref[c] = kk.T.astype(BF16)
        gate_rows(gt_ref, gs_ref, c)
        return carry

    lax.fori_loop(0, n_lat, prep_lat, 0, unroll=4)

    def prep_ctx(c, carry):
        kk = conv_silu(kc_ref, c, n_ctx, cwk_ref, cbk_ref) * k_scale
        kstc_ref[c] = kk.T.astype(BF16)
        gate_rows(gtc_ref, gsc_ref, c)
        return carry

    lax.fori_loop(0, n_ctx, prep_ctx, 0)

    def vaug(vref, c):
        r0 = pl.multiple_of(c * CHUNK, CHUNK)
        return jnp.concatenate([vref[pl.ds(r0, CHUNK), :], ones_col], axis=1)

    def state_update(d, m_prev, a_row, lf_row, kt, va):
        m_last = jnp.maximum(m_prev, jnp.max(a_row, axis=1, keepdims=True))
        b_last = jnp.sum(lf_row, axis=1, keepdims=True)
        w_row = jnp.exp(a_row - m_last)
        decay = jnp.exp(m_prev - m_last)
        kw = (kt.astype(F32) * w_row).astype(BF16)
        cst_ref[d] = decay * cst_ref[d] + _dot(kw, va)
        return b_last + m_last

    def chunk_step(d, c, m_prev, mask):
        a_row = gs_ref[c, 2 * d:2 * d + 1, :]
        lf_row = gs_ref[c, 2 * d + 1:2 * d + 2, :]
        q = qs_ref[c]
        kt = kst_ref[c]
        va = vaug(v_ref, c)
        a_mat = jnp.where(mask, a_row, -jnp.inf)
        m_col = jnp.maximum(jnp.max(a_mat, axis=1, keepdims=True), m_prev)
        w_mat = jnp.exp(a_mat - m_col)
        w_inter = jnp.exp(m_prev - m_col)
        b_col = jnp.sum(jnp.where(mask, lf_row, 0.0), axis=1, keepdims=True)
        s = (_dot(q, kt) * w_mat).astype(BF16)
        nd = w_inter * _dot(q, cst_ref[d].astype(BF16)) + _dot(s, va)
        num = nd[:, :HEAD_DIM]
        den = nd[:, HEAD_DIM:HEAD_DIM + 1]
        hh = num / jnp.maximum(jnp.abs(den), jnp.exp(-b_col - m_col))
        m_new = state_update(d, m_prev, a_row, lf_row, kt, va)
        return hh, m_new

    cst_ref[...] = jnp.zeros(cst_ref.shape, F32)

    def ctx_fwd(c, m):
        return state_update(0, m, gsc_ref[c, 0:1, :], gsc_ref[c, 1:2, :], kstc_ref[c], vaug(vc_ref, c))

    def ctx_bwd(i, m):
        c = n_ctx - 1 - i
        return state_update(1, m, gsc_ref[c, 2:3, :], gsc_ref[c, 3:4, :], kstc_ref[c], vaug(vc_ref, c))

    m_f = lax.fori_loop(0, n_ctx, ctx_fwd, jnp.zeros((1, 1), F32))
    m_b = lax.fori_loop(0, n_ctx, ctx_bwd, jnp.zeros((1, 1), F32))

    def main_step(i, carry):
        m_f, m_b = carry
        hf, m_f = chunk_step(0, i, m_f, tril)
        hf_ref[i] = hf
        cb = n_lat - 1 - i
        hb, m_b = chunk_step(1, cb, m_b, triu)
        hb_ref[cb] = hb
        return m_f, m_b

    lax.fori_loop(0, n_lat, main_step, (m_f, m_b), unroll=8)

    def finish(c, carry):
        r0 = pl.multiple_of(c * CHUNK, CHUNK)
        hh = _rms(hf_ref[c] + hb_ref[c], gn_ref[...])
        og = o_ref[pl.ds(r0, CHUNK), :].astype(F32)
        out_ref[pl.ds(r0, CHUNK), :] = (hh * jax.nn.sigmoid(og)).astype(BF16)
        return carry

    lax.fori_loop(0, n_lat, finish, 0, unroll=4)


def _mlstm_call(bg, p_lat, gt_lat, p_ctx, gt_ctx, cw, cb, gn, *, batch, seq):
    n_lat = seq // CHUNK
    n_ctx = CTX_LEN // CHUNK

    def col(tile):
        return lambda b, h: (b, tile * HEADS + h)

    lat_blk = (seq, HEAD_DIM)
    ctx_blk = (CTX_LEN, HEAD_DIM)
    return pl.pallas_call(
        _mlstm_kernel,
        grid=(batch, HEADS),
        in_specs=[
            pl.BlockSpec(memory_space=pltpu.SMEM),
            pl.BlockSpec(lat_blk, col(TILE_MQ)),
            pl.BlockSpec(lat_blk, col(TILE_MK)),
            pl.BlockSpec(lat_blk, col(TILE_MV)),
            pl.BlockSpec(lat_blk, col(TILE_MO)),
            pl.BlockSpec((n_lat, N_GATES, CHUNK), lambda b, h: (b, 0, 0)),
            pl.BlockSpec(ctx_blk, col(TILE_MK)),
            pl.BlockSpec(ctx_blk, col(TILE_MV)),
            pl.BlockSpec((n_ctx, N_GATES, CHUNK), lambda b, h: (b, 0, 0)),
            pl.BlockSpec((3, HEAD_DIM), lambda b, h: (0, h)),
            pl.BlockSpec((3, HEAD_DIM), lambda b, h: (0, HEADS + h)),
            pl.BlockSpec((1, HEAD_DIM), lambda b, h: (0, h)),
            pl.BlockSpec((1, HEAD_DIM), lambda b, h: (0, HEADS + h)),
            pl.BlockSpec((1, HEAD_DIM), lambda b, h: (0, h)),
        ],
        out_specs=pl.BlockSpec(lat_blk, lambda b, h: (b, h)),
        out_shape=jax.ShapeDtypeStruct((batch * seq, GROUP_WIDTH), BF16),
        scratch_shapes=[
            pltpu.VMEM((n_lat, CHUNK, HEAD_DIM), BF16),
            pltpu.VMEM((n_lat, HEAD_DIM, CHUNK), BF16),
            pltpu.VMEM((n_lat, 8, CHUNK), F32),
            pltpu.VMEM((n_ctx, HEAD_DIM, CHUNK), BF16),
            pltpu.VMEM((n_ctx, 8, CHUNK), F32),
            pltpu.VMEM((n_lat, CHUNK, HEAD_DIM), F32),
            pltpu.VMEM((n_lat, CHUNK, HEAD_DIM), F32),
            pltpu.VMEM((2, HEAD_DIM, 2 * HEAD_DIM), F32),
        ],
        compiler_params=_cparams(("parallel", "parallel")),
        name="mlstm",
    )(bg, p_lat, p_lat, p_lat, p_lat, gt_lat, p_ctx, p_ctx, gt_ctx, cw, cw, cb, cb, gn)


def _attn_kernel(q_ref, k_ref, vt_ref, kc_ref, vtc_ref, lam_ref, gd_ref, o_ref,
                 qq_ref, m_ref, acc_ref, s0_ref, s1_ref, cm0_ref, cm1_ref,
                 p0_ref, p1_ref, al0_ref, al1_ref, l_ref, *, lam_init):
    tq = q_ref.shape[0]
    n_kv = k_ref.shape[0] // ATT_TK

    lane = lax.broadcasted_iota(jnp.int32, (tq, HEAD_DIM), 1)
    q = q_ref[...]
    zero = jnp.zeros_like(q)
    qq_ref[0:tq, :] = jnp.where(lane < DIFF_QK_DIM, q, zero)
    qq_ref[tq:2 * tq, :] = jnp.where(lane >= DIFF_QK_DIM, q, zero)

    sbufs = ((s0_ref, cm0_ref), (s1_ref, cm1_ref))
    pbufs = ((p0_ref, al0_ref), (p1_ref, al1_ref))
    strips = [slice(c * ATT_STRIP, (c + 1) * ATT_STRIP) for c in range(2 * tq // ATT_STRIP)]

    def k_lat(t):
        return k_ref[pl.ds(pl.multiple_of(t * ATT_TK, ATT_TK), ATT_TK), :]

    def with_ones(vt):
        return jnp.concatenate([vt, jnp.ones((BF16_SUBLANES, vt.shape[1]), BF16)], axis=0)

    k_abs = jnp.maximum(jnp.max(jnp.abs(k_ref[...]), axis=0, keepdims=True),
                        jnp.max(jnp.abs(kc_ref[...]), axis=0, keepdims=True))
    bound = _dot_nt(jnp.broadcast_to(k_abs, (BF16_SUBLANES, HEAD_DIM)), jnp.abs(qq_ref[...]))
    m_ref[...] = bound[0:1, :] - SHIFT_HEADROOM_LOG2
    acc_ref[...] = jnp.zeros(acc_ref.shape, F32)

    l_ref[...] = jnp.zeros(l_ref.shape, F32)

    def shifted_weights(k, buf, cs):
        st = _dot_nt(k, qq_ref[cs, :])
        w = jnp.exp2(st - m_ref[:, cs])
        l_ref[:, cs] += jnp.sum(w, axis=0, keepdims=True)
        pbufs[buf][0][0:k.shape[0], cs] = w.astype(BF16)

    def accumulate(vt, buf, cs):
        acc_ref[0:HEAD_DIM, cs] += _dot(vt, pbufs[buf][0][0:vt.shape[1], cs])

    def fixed_substep(k, w_buf, vt, pv_buf):
        for cs in strips:
            if k is not None:
                shifted_weights(k, w_buf, cs)
            if vt is not None:
                accumulate(vt, pv_buf, cs)

    fixed_substep(kc_ref[...], 0, None, None)
    fixed_substep(k_lat(0), 1, vtc_ref[0], 0)

    def fixed_body(i, carry):
        t = 2 * i
        fixed_substep(k_lat(t + 1), 0, vt_ref[t], 1)
        fixed_substep(k_lat(t + 2), 1, vt_ref[t + 1], 0)
        return carry

    lax.fori_loop(0, n_kv // 2 - 1, fixed_body, 0)
    fixed_substep(k_lat(n_kv - 1), 0, vt_ref[n_kv - 2], 1)
    fixed_substep(None, None, vt_ref[n_kv - 1], 0)

    l_min = jnp.min(l_ref[...])
    in_range = l_min >= 2.0 ** -SHIFT_HEADROOM_LOG2

    def score(k, buf, cs):
        s_ref, cm_ref = sbufs[buf]
        st = _dot_nt(k, qq_ref[cs, :])
        s_ref[0:k.shape[0], cs] = st
        cm_ref[:, cs] = jnp.max(st, axis=0, keepdims=True)

    def softmax(buf, rows, cs):
        s_ref, cm_ref = sbufs[buf]
        p_ref, al_ref = pbufs[buf]
        m_prev = m_ref[:, cs]
        m_new = jnp.maximum(m_prev, cm_ref[:, cs])
        p_ref[0:rows, cs] = jnp.exp2((s_ref[0:rows, cs] - m_new).astype(BF16))
        al_ref[:, cs] = jnp.exp2(m_prev - m_new)
        m_ref[:, cs] = m_new

    def pv(vt_aug, buf, cs):
        p_ref, al_ref = pbufs[buf]
        rows = vt_aug.shape[1]
        acc_ref[:, cs] = al_ref[:, cs] * acc_ref[:, cs] + _dot(vt_aug, p_ref[0:rows, cs])

    def substep(k, score_buf, soft_buf, soft_rows, vt, pv_buf):
        if vt is not None:
            vt_aug = with_ones(vt)
        for cs in strips:
            if k is not None:
                score(k, score_buf, cs)
            if soft_buf is not None:
                softmax(soft_buf, soft_rows, cs)
            if vt is not None:
                pv(vt_aug, pv_buf, cs)

    def body(i, carry):
        t = 2 * i
        substep(k_lat(t + 2), 1, 0, ATT_TK, vt_ref[t], 1)
        substep(k_lat(t + 3), 0, 1, ATT_TK, vt_ref[t + 1], 0)
        return carry

    @pl.when(jnp.logical_not(in_range))
    def _():
        m_ref[...] = jnp.full(m_ref.shape, -jnp.inf, F32)
        acc_ref[...] = jnp.zeros(acc_ref.shape, F32)
        substep(kc_ref[...], 0, None, None, None, None)
        substep(k_lat(0), 1, 0, CTX_LEN, None, None)
        substep(k_lat(1), 0, 1, ATT_TK, vtc_ref[0], 0)
        lax.fori_loop(0, n_kv // 2 - 1, body, 0)
        substep(None, None, 0, ATT_TK, vt_ref[n_kv - 2], 1)
        substep(None, None, None, None, vt_ref[n_kv - 1], 0)
        l_ref[...] = acc_ref[HEAD_DIM:HEAD_DIM + 1, :]

    lp = lam_ref[...]
    e1 = jnp.exp(jnp.sum(lp[0:1, :] * lp[1:2, :], axis=1, keepdims=True))
    e2 = jnp.exp(jnp.sum(lp[2:3, :] * lp[3:4, :], axis=1, keepdims=True))
    lam = e1 - e2 + lam_init
    o = acc_ref[0:HEAD_DIM, :] / l_ref[...]
    od = o[:, 0:tq] - lam * o[:, tq:2 * tq]
    od = od * lax.rsqrt(jnp.mean(od * od, axis=0, keepdims=True) + EPS)
    o_ref[...] = (od.T * gd_ref[...] * (1.0 - lam_init)).astype(BF16)


def _attn_call(p_lat, vt_lat, p_ctx, vt_ctx, lamp, gd, *, batch, seq, lam_init):
    nq = seq // ATT_TQ
    n_kv = seq // ATT_TK
    kern = functools.partial(_attn_kernel, lam_init=lam_init)
    return pl.pallas_call(
        kern,
        grid=(batch, HEADS, nq),
        in_specs=[
            pl.BlockSpec((ATT_TQ, HEAD_DIM), lambda b, h, i: (b * nq + i, TILE_DQ * HEADS + h)),
            pl.BlockSpec((seq, HEAD_DIM), lambda b, h, i: (b, TILE_DK * HEADS + h)),
            pl.BlockSpec((n_kv, HEAD_DIM, ATT_TK), lambda b, h, i: (b, h, 0)),
            pl.BlockSpec((CTX_LEN, HEAD_DIM), lambda b, h, i: (b, TILE_DK * HEADS + h)),
            pl.BlockSpec((1, HEAD_DIM, CTX_LEN), lambda b, h, i: (b, h, 0)),
            pl.BlockSpec((8, LANES), lambda b, h, i: (0, 0)),
            pl.BlockSpec((1, HEAD_DIM), lambda b, h, i: (0, 0)),
        ],
        out_specs=pl.BlockSpec((ATT_TQ, HEAD_DIM), lambda b, h, i: (b * nq + i, h)),
        out_shape=jax.ShapeDtypeStruct((batch * seq, GROUP_WIDTH), BF16),
        scratch_shapes=[
            pltpu.VMEM((2 * ATT_TQ, HEAD_DIM), BF16),
            pltpu.VMEM((1, 2 * ATT_TQ), F32),
            pltpu.VMEM((HEAD_DIM + BF16_SUBLANES, 2 * ATT_TQ), F32),
            pltpu.VMEM((ATT_TK, 2 * ATT_TQ), F32),
            pltpu.VMEM((ATT_TK, 2 * ATT_TQ), F32),
            pltpu.VMEM((1, 2 * ATT_TQ), F32),
            pltpu.VMEM((1, 2 * ATT_TQ), F32),
            pltpu.VMEM((ATT_TK, 2 * ATT_TQ), BF16),
            pltpu.VMEM((ATT_TK, 2 * ATT_TQ), BF16),
            pltpu.VMEM((1, 2 * ATT_TQ), F32),
            pltpu.VMEM((1, 2 * ATT_TQ), F32),
            pltpu.VMEM((1, 2 * ATT_TQ), F32),
        ],
        compiler_params=_cparams(("parallel", "parallel", "arbitrary")),
        name="diffattn",
    )(p_lat, p_lat, vt_lat, p_ctx, vt_ctx, lamp, gd)


def _outproj_kernel(m_ref, d_ref, wa_ref, wb_ref, x_ref, ga_ref, sh_ref, sc_ref, gpost_ref, gpre_ref,
                    x1_ref, h_ref):
    y = _dot(m_ref[...], wa_ref[...]) + _dot(d_ref[...], wb_ref[...])
    x1 = x_ref[...] + ga_ref[0] * _rms(y, gpost_ref[...])
    x1_ref[...] = x1
    hh = _rms(x1, gpre_ref[...]) * (1.0 + sc_ref[0]) + sh_ref[0]
    h_ref[...] = hh.astype(BF16)


def _outproj_call(m_lat, d_lat, wa, wb, x2d, ga1, sh2, sc2, gpost, gpre, *, tm, seq):
    m = x2d.shape[0]
    bpb = seq // tm
    row = lambda i: (i, 0)
    fixed = lambda i: (0, 0)
    per_batch = lambda i: (i // bpb, 0, 0)
    return pl.pallas_call(
        _outproj_kernel,
        grid=(m // tm,),
        in_specs=[
            pl.BlockSpec((tm, GROUP_WIDTH), row),
            pl.BlockSpec((tm, GROUP_WIDTH), row),
            pl.BlockSpec((GROUP_WIDTH, D_MODEL), fixed),
            pl.BlockSpec((GROUP_WIDTH, D_MODEL), fixed),
            pl.BlockSpec((tm, D_MODEL), row),
            pl.BlockSpec((1, 1, D_MODEL), per_batch),
            pl.BlockSpec((1, 1, D_MODEL), per_batch),
            pl.BlockSpec((1, 1, D_MODEL), per_batch),
            pl.BlockSpec((1, D_MODEL), fixed),
            pl.BlockSpec((1, D_MODEL), fixed),
        ],
        out_specs=[pl.BlockSpec((tm, D_MODEL), row), pl.BlockSpec((tm, D_MODEL), row)],
        out_shape=[jax.ShapeDtypeStruct((m, D_MODEL), F32), jax.ShapeDtypeStruct((m, D_MODEL), BF16)],
        compiler_params=_cparams(("parallel",)),
        name="outproj",
    )(m_lat, d_lat, wa, wb, x2d, ga1, sh2, sc2, gpost, gpre)


def _ffn_kernel(h_ref, hp_ref, hn_ref, wg_ref, wv_ref, cwg_ref, cwv_ref, cbg_ref, cbv_ref, wd_ref,
                x1_ref, ga_ref, gpost_ref, o_ref, lhs_ref, acc_ref, ug_ref, uv_ref, a_ref, *, blocks_per_seq):
    i = pl.program_id(0)
    f = pl.program_id(1)
    tm = h_ref.shape[0]
    halo = BF16_SUBLANES

    @pl.when(f == 0)
    def _():
        lhs_ref[0:halo, :] = hp_ref[...]
        lhs_ref[halo:halo + tm, :] = h_ref[...]
        lhs_ref[halo + tm:2 * halo + tm, :] = hn_ref[...]
        acc_ref[...] = jnp.zeros(acc_ref.shape, F32)

    has_prev = ((i % blocks_per_seq) > 0).astype(F32)
    has_next = ((i % blocks_per_seq) < blocks_per_seq - 1).astype(F32)

    def up_project(w_ref, u_ref):
        u_ref[...] = _dot(lhs_ref[...], w_ref[...])
        u_ref[halo - 1:halo, :] = u_ref[halo - 1:halo, :] * has_prev
        u_ref[halo + tm:halo + tm + 1, :] = u_ref[halo + tm:halo + tm + 1, :] * has_next

    up_project(wg_ref, ug_ref)
    up_project(wv_ref, uv_ref)

    def conv(u_ref, r, cw, cb):
        lo = halo + r
        return (u_ref[lo - 1:lo - 1 + FFN_ROWS, :] * cw[0:1, :] + u_ref[lo:lo + FFN_ROWS, :] * cw[1:2, :]
                + u_ref[lo + 1:lo + 1 + FFN_ROWS, :] * cw[2:3, :] + cb)

    cwg, cwv, cbg, cbv = cwg_ref[...], cwv_ref[...], cbg_ref[...], cbv_ref[...]
    half = tm // 2
    for r0 in (0, half):
        for r in range(r0, r0 + half, FFN_ROWS):
            gate = conv(ug_ref, r, cwg, cbg)
            val = conv(uv_ref, r, cwv, cbv)
            a_ref[r:r + FFN_ROWS, :] = (gate * jax.nn.sigmoid(gate) * val).astype(BF16)
        acc_ref[r0:r0 + half, :] += _dot(a_ref[r0:r0 + half, :], wd_ref[...])

    @pl.when(f == pl.num_programs(1) - 1)
    def _():
        o_ref[...] = x1_ref[...] + ga_ref[0] * _rms(acc_ref[...], gpost_ref[...])


def _ffn_call(hff, w_up, cw, cb, w_down, x1, ga2, gpost, *, tm, tf, seq):
    m = hff.shape[0]
    nf = D_FF // tf
    bps = seq // tm
    halo = BF16_SUBLANES
    hb = tm // halo
    n_halo = m // halo
    kern = functools.partial(_ffn_kernel, blocks_per_seq=bps)
    row = lambda i, f: (i, 0)
    fixed = lambda i, f: (0, 0)
    return pl.pallas_call(
        kern,
        grid=(m // tm, nf),
        in_specs=[
            pl.BlockSpec((tm, D_MODEL), row),
            pl.BlockSpec((halo, D_MODEL), lambda i, f: (jnp.maximum(i * hb - 1, 0), 0)),
            pl.BlockSpec((halo, D_MODEL), lambda i, f: (jnp.minimum((i + 1) * hb, n_halo - 1), 0)),
            pl.BlockSpec((D_MODEL, tf), lambda i, f: (0, f)),
            pl.BlockSpec((D_MODEL, tf), lambda i, f: (0, nf + f)),
            pl.BlockSpec((3, tf), lambda i, f: (0, f)),
            pl.BlockSpec((3, tf), lambda i, f: (0, nf + f)),
            pl.BlockSpec((1, tf), lambda i, f: (0, f)),
            pl.BlockSpec((1, tf), lambda i, f: (0, nf + f)),
            pl.BlockSpec((tf, D_MODEL), lambda i, f: (f, 0)),
            pl.BlockSpec((tm, D_MODEL), row),
            pl.BlockSpec((1, 1, D_MODEL), lambda i, f: (i // bps, 0, 0)),
            pl.BlockSpec((1, D_MODEL), fixed),
        ],
        out_specs=pl.BlockSpec((tm, D_MODEL), row),
        out_shape=jax.ShapeDtypeStruct((m, D_MODEL), F32),
        scratch_shapes=[pltpu.VMEM((tm + 2 * halo, D_MODEL), BF16), pltpu.VMEM((tm, D_MODEL), F32),
                        pltpu.VMEM((tm + 2 * halo, tf), F32), pltpu.VMEM((tm + 2 * halo, tf), F32),
                        pltpu.VMEM((tm, tf), BF16)],
        compiler_params=_cparams(("parallel", "arbitrary")),
        name="convffn",
    )(hff, hff, hff, w_up, w_up, cw, cw, cb, cb, w_down, x1, ga2, gpost)


def _rope_tables(seq):
    rows = seq // GRID_W
    r = np.repeat(np.arange(rows, dtype=np.float64), GRID_W)
    c = np.tile(np.arange(GRID_W, dtype=np.float64), rows)
    inv_freq = ROPE_BASE ** (-np.arange(ROPE_FREQS, dtype=np.float64) / ROPE_FREQS)
    ar = r[:, None] * inv_freq
    ac = c[:, None] * inv_freq
    ang = np.concatenate([ar, ar, ac, ac, ar, ar, ac, ac], axis=1)
    sign = np.tile(np.concatenate([-np.ones(ROPE_FREQS), np.ones(ROPE_FREQS)]), 4)
    return jnp.asarray(np.cos(ang), F32), jnp.asarray(np.sin(ang) * sign, F32)


def kernel(x, c, ctx, c_ctx, w_mod, b_mod, g_pre_mix, g_post_mix, w_in, b_gate, conv_qk_w, conv_qk_b,
           g_mlstm, lambda_q1, lambda_k1, lambda_q2, lambda_k2, g_diff, w_out, g_pre_ffn, g_post_ffn,
           w_up, conv_ffn_w, conv_ffn_b, w_down):
    batch, seq, d = x.shape
    depth = w_mod.shape[0]
    assert depth == 1 and d == D_MODEL and ctx.shape[1] == CTX_LEN
    l = 0
    lam_init = 0.8 - 0.6 * math.exp(-0.3 * l)

    cc = jnp.zeros((8, d), F32).at[:batch].set(c).at[batch].set(c_ctx)
    mod = _mod_call(cc, w_mod[l], b_mod[l][None, :])
    mod = mod.reshape(8, N_MOD, 1, d)
    sh1, sc1, ga1, sh2, sc2, ga2 = (mod[:batch, k] for k in range(N_MOD))
    csh1 = jnp.broadcast_to(mod[batch, 0][None], (batch, 1, d))
    csc1 = jnp.broadcast_to(mod[batch, 1][None], (batch, 1, d))

    wt = w_in[l].T
    wp = jnp.concatenate([wt[:OFF_MG], wt[OFF_DQ:]], axis=0).astype(BF16)
    wgt = wt[OFF_MG:OFF_DQ].astype(BF16)

    cos_t, sin_t = _rope_tables(seq)
    g_pre = g_pre_mix[l][None, :]
    x2d = x.reshape(batch * seq, d)
    ctx2d = ctx.reshape(batch * CTX_LEN, d)

    p_lat, vt_lat, gt_lat = _inproj_call(x2d, g_pre, sh1, sc1, wp, wgt, cos_t, sin_t,
                                         tm=512, rows_per_batch=seq, rope=True, tk_v=ATT_TK)
    p_ctx, vt_ctx, gt_ctx = _inproj_call(ctx2d, g_pre, csh1, csc1, wp, wgt, cos_t, sin_t,
                                         tm=batch * CTX_LEN, rows_per_batch=CTX_LEN, rope=False, tk_v=CTX_LEN)

    m_lat = _mlstm_call(b_gate[l], p_lat, gt_lat, p_ctx, gt_ctx, conv_qk_w[l], conv_qk_b[l][None, :],
                        g_mlstm[l][None, :], batch=batch, seq=seq)

    lamp = jnp.zeros((8, LANES), F32)
    lamp = lamp.at[0, :DIFF_QK_DIM].set(lambda_q1[l]).at[1, :DIFF_QK_DIM].set(lambda_k1[l])
    lamp = lamp.at[2, :DIFF_QK_DIM].set(lambda_q2[l]).at[3, :DIFF_QK_DIM].set(lambda_k2[l])
    d_lat = _attn_call(p_lat, vt_lat, p_ctx, vt_ctx, lamp, g_diff[l][None, :],
                       batch=batch, seq=seq, lam_init=lam_init)

    wo = w_out[l].astype(BF16)
    x1, hff = _outproj_call(m_lat, d_lat, wo[:GROUP_WIDTH], wo[GROUP_WIDTH:], x2d, ga1, sh2, sc2,
                            g_post_mix[l][None, :], g_pre_ffn[l][None, :], tm=512, seq=seq)

    x2 = _ffn_call(hff, w_up[l].astype(BF16), conv_ffn_w[l], conv_ffn_b[l][None, :],
                   w_down[l].astype(BF16), x1, ga2, g_post_ffn[l][None, :], tm=512, tf=512, seq=seq)
    return x2.reshape(batch, seq, d)
```
